```python
import math
import jax
import jax.numpy as jnp
from jax import lax
import numpy as np

D_MODEL = 2048
BATCH = 32
SEQ = 256
DEPTH = 2
DEC_BATCH = 8
DEC_SEQ = 4096
PAST_LEN = 256

GRID_W = 64
GROUP_W = D_MODEL // 4
MIX_W = 4 * GROUP_W
HY_W = GROUP_W
HY_BANDS = 16
HY_EMB = 1 + 2 * HY_BANDS
HY_FH = 64
HY_SHORT = 3
HY_MIN_DECAY = math.log(100.0) / 1.5
HY_MAX_DECAY = math.log(100.0) / 0.3
RG_W = GROUP_W
RG_H = 4
RG_DH = RG_W // RG_H
RG_CONV = 4
RG_PAD = ((RG_CONV - 1) // 2, RG_CONV // 2)
RG_C = 8.0
HG_W = GROUP_W
HG_H = 4
HG_DK = HG_W // HG_H
HG_DV = HG_W // HG_H
HG_CHUNK = 32
ATT_W = GROUP_W
HEAD_DIM = 64
ATT_H = ATT_W // HEAD_DIM
ATT_KV = 2
ATT_G = ATT_H // ATT_KV
WINDOW = 128
ATT_BLOCK = 128
ATT_SCALE = 1.0 / math.sqrt(HEAD_DIM)
ROPE_BASE = 10000.0
NEG_INF = -1e30
D_FF = 5632
FFN_CONV = 3
IN_SIZES = (3 * HY_W, RG_W, RG_W, HG_W, HG_W, HG_W, HG_W, HG_W, ATT_W, ATT_KV * HEAD_DIM, ATT_KV * HEAD_DIM)
IN_COLS = sum(IN_SIZES)

kernel_name = 'hybrid_prefix_diffusion_step'


def rms_norm(x, g, eps=1e-6):
    xf = x.astype(jnp.float32)
    y = xf * lax.rsqrt(jnp.mean(xf * xf, axis=-1, keepdims=True) + eps)
    return (y * g.astype(jnp.float32)).astype(x.dtype)


def dwconv(x, w, b, pad):
    y = lax.conv_general_dilated(x, w.astype(x.dtype)[:, None, :], window_strides=(1,), padding=[pad],
                                 dimension_numbers=('NWC', 'WIO', 'NWC'), feature_group_count=x.shape[-1])
    return y + b.astype(x.dtype)


def hyena_kernel(L, w1, b1, freq, w2, b2, w3, b3, decay):
    f32 = jnp.float32
    t_idx = jnp.arange(L, dtype=f32)
    t = t_idx / (L - 1)
    ang = (2.0 * math.pi * t_idx / L)[:, None] * jnp.linspace(1e-4, HY_BANDS - 1, HY_BANDS, dtype=f32)[None, :]
    z = jnp.concatenate([t[:, None], jnp.cos(ang), -jnp.sin(ang)], axis=-1)
    h = jnp.sin(freq[0].astype(f32) * (z @ w1.astype(f32) + b1.astype(f32)))
    h = jnp.sin(freq[1].astype(f32) * (h @ w2.astype(f32) + b2.astype(f32)))
    h = (h @ w3.astype(f32) + b3.astype(f32)) * jnp.exp(-t[:, None] * jnp.abs(decay.astype(f32)))
    h_fwd, h_bwd = h[:, :HY_W], h[:, HY_W:]
    k = jnp.concatenate([h_fwd, jnp.zeros((1, HY_W), f32), h_bwd[:0:-1]], axis=0)
    return k / jnp.sum(jnp.abs(k), axis=0, keepdims=True)


def centred_long_conv(u, k, skip):
    L = u.shape[1]
    spec = jnp.fft.rfft(u, n=2 * L, axis=1) * jnp.fft.rfft(k, axis=0)[None]
    return jnp.fft.irfft(spec, n=2 * L, axis=1)[:, :L] + u * skip


def _affine_combine(e1, e2):
    a1, b1 = e1
    a2, b2 = e2
    return a1 * a2, a2 * b1 + b2


def rglru_scan(x, gate_w, gate_b, lam, h0):
    f32 = jnp.float32
    B, T, W = x.shape
    xh = x.reshape(B, T, RG_H, RG_DH)
    gates = jnp.einsum('bthi,ghij->gbthj', xh, gate_w.astype(f32)).reshape(2, B, T, W) + gate_b.astype(f32)[:, None, None, :]
    r = jax.nn.sigmoid(gates[0])
    i = jax.nn.sigmoid(gates[1])
    log_a = -RG_C * r * jax.nn.softplus(-lam.astype(f32))
    a = jnp.exp(log_a)
    b = jnp.sqrt(-jnp.expm1(2.0 * log_a)) * (i * x)
    b = b.at[:, 0].add(a[:, 0] * h0)
    _, h = lax.associative_scan(_affine_combine, (a, b), axis=1)
    return h, h[:, -1]


def hgrn2_scan(q, k, v, log_f, S0):
    B, T, H, Dk = q.shape
    Dv = v.shape[-1]
    N = T // HG_CHUNK

    def chunks(a):
        return a.reshape(B, N, HG_CHUNK, H, a.shape[-1]).transpose(1, 0, 3, 2, 4)

    causal = jnp.tril(jnp.ones((HG_CHUNK, HG_CHUNK), jnp.float32))
    mid = HG_CHUNK // 2 - 1

    def step(S, inp):
        qc, kc, vc, gc = inp
        bc = jnp.cumsum(gc, axis=2)
        ref = bc[:, :, mid:mid + 1]
        att = jnp.einsum('bhtd,bhsd->bhts', qc * jnp.exp(bc - ref), kc * jnp.exp(ref - bc)) * causal
        o = jnp.einsum('bhts,bhse->bhte', att, vc) + jnp.einsum('bhtd,bhde->bhte', qc * jnp.exp(bc), S)
        bl = bc[:, :, -1:]
        S = jnp.exp(bl)[:, :, 0, :, None] * S + jnp.einsum('bhsd,bhse->bhde', kc * jnp.exp(bl - bc), vc)
        return S, o

    S, o = lax.scan(step, S0, (chunks(q), chunks(k), chunks(v), chunks(log_f)))
    return o.transpose(1, 0, 3, 2, 4).reshape(B, T, H, Dv), S


def axial_rope(x):
    f32 = jnp.float32
    T = x.shape[1]
    rows = T // GRID_W
    row = jnp.repeat(jnp.arange(rows, dtype=f32), GRID_W)
    col = jnp.tile(jnp.arange(GRID_W, dtype=f32), rows)
    half = HEAD_DIM // 2
    inv = ROPE_BASE ** (-jnp.arange(0, half, 2, dtype=f32) / half)

    def rot(xa, pos):
        ang = pos[:, None] * inv[None, :]
        cos = jnp.cos(ang)[None, :, None, :]
        sin = jnp.sin(ang)[None, :, None, :]
        x1, x2 = jnp.split(xa.astype(f32), 2, axis=-1)
        return jnp.concatenate([x1 * cos - x2 * sin, x1 * sin + x2 * cos], axis=-1)

    return jnp.concatenate([rot(x[..., :half], row), rot(x[..., half:], col)], axis=-1).astype(x.dtype)


def sink_attend(qb, keys, vals, bias, sink):
    s = jnp.einsum('bqkgd,bskd->bkgqs', qb, keys).astype(jnp.float32) * ATT_SCALE + bias
    sk = sink[None, :, :, None, None]
    m = jnp.maximum(jnp.max(s, axis=-1, keepdims=True), sk)
    e = jnp.exp(s - m)
    p = e / (jnp.sum(e, axis=-1, keepdims=True) + jnp.exp(sk - m))
    return jnp.einsum('bkgqs,bskd->bqkgd', p.astype(vals.dtype), vals)


def attn_context(q, k, v, sink):
    B, S = q.shape[:2]
    nb = S // ATT_BLOCK
    qb = q.reshape(B, nb, ATT_BLOCK, ATT_KV, ATT_G, HEAD_DIM).swapaxes(0, 1)
    o = lax.map(lambda qq: sink_attend(qq, k, v, 0.0, sink), qb)
    return o.swapaxes(0, 1).reshape(B, S, ATT_W)


def attn_latent(q, k, v, kc, vc, sink):
    B, T = q.shape[:2]
    nb = T // ATT_BLOCK
    P = kc.shape[1]

    def blocks3(a):
        ap = jnp.pad(a, ((0, 0), (ATT_BLOCK, ATT_BLOCK), (0, 0), (0, 0))).reshape(B, nb + 2, ATT_BLOCK, ATT_KV, HEAD_DIM)
        return jnp.concatenate([ap[:, :-2], ap[:, 1:-1], ap[:, 2:]], axis=2).swapaxes(0, 1)

    kl, vl = blocks3(k), blocks3(v)
    qb = q.reshape(B, nb, ATT_BLOCK, ATT_KV, ATT_G, HEAD_DIM).swapaxes(0, 1)
    base = jnp.arange(nb)[:, None, None] * ATT_BLOCK
    qpos = base + jnp.arange(ATT_BLOCK)[None, :, None]
    kpos = base - ATT_BLOCK + jnp.arange(3 * ATT_BLOCK)[None, None, :]
    ok = (jnp.abs(qpos - kpos) <= WINDOW) & (kpos >= 0) & (kpos < T)
    bias = jnp.concatenate([jnp.where(ok, 0.0, NEG_INF).astype(jnp.float32),
                            jnp.zeros((nb, ATT_BLOCK, P), jnp.float32)], axis=-1)

    def one(args):
        qq, kk, vv, bb = args
        return sink_attend(qq, jnp.concatenate([kk, kc], axis=1), jnp.concatenate([vv, vc], axis=1), bb, sink)

    o = lax.map(one, (qb, kl, vl, bias))
    return o.swapaxes(0, 1).reshape(B, T, ATT_W)


def token_mixers(u, lp, lb, ctx):
    f32 = jnp.float32
    B, T, _ = u.shape
    z = u @ lp['w_in']
    (hy_in, rg_x, rg_gate, hg_q, hg_i, hg_ff, hg_fb, hg_gate, a_q, a_k, a_v) = jnp.split(
        z, np.cumsum(IN_SIZES)[:-1].tolist(), axis=-1)

    hy = dwconv(hy_in, lp['hy_conv_w'], lp['hy_conv_b'], (HY_SHORT // 2, HY_SHORT // 2)).astype(f32)
    hv, hx1, hx2 = jnp.split(hy, 3, axis=-1)
    k_long = hyena_kernel(T, lp['hy_f_w1'], lp['hy_f_b1'], lp['hy_f_freq'], lp['hy_f_w2'], lp['hy_f_b2'],
                          lp['hy_f_w3'], lp['hy_f_b3'], lp['hy_decay'])
    y_hy = hx1 * centred_long_conv(hv * hx2, k_long, lp['hy_skip'].astype(f32))

    rx = dwconv(rg_x, lp['rg_conv_w'], lp['rg_conv_b'], RG_PAD).astype(f32)
    h0 = jnp.zeros((B, 2, RG_W), f32) if ctx is None else ctx[2].astype(f32)
    h_f, hT_f = rglru_scan(rx, lp['rg_gate_w'][0], lp['rg_gate_b'][0], lp['rg_lambda'][0], h0[:, 0])
    h_b, hT_b = rglru_scan(rx[:, ::-1], lp['rg_gate_w'][1], lp['rg_gate_b'][1], lp['rg_lambda'][1], h0[:, 1])
    y_rg = (h_f + h_b[:, ::-1]) * jax.nn.gelu(rg_gate.astype(f32))

    S0 = jnp.zeros((B, 2, HG_H, HG_DK, HG_DV), f32) if ctx is None else ctx[3].astype(f32)
    q = hg_q.astype(f32).reshape(B, T, HG_H, HG_DK)
    vv = hg_i.astype(f32).reshape(B, T, HG_H, HG_DV)
    f_f = (lb[0] + (1.0 - lb[0]) * jax.nn.sigmoid(hg_ff.astype(f32))).reshape(B, T, HG_H, HG_DK)
    f_b = (lb[1] + (1.0 - lb[1]) * jax.nn.sigmoid(hg_fb.astype(f32))).reshape(B, T, HG_H, HG_DK)
    o_f, S_f = hgrn2_scan(q, 1.0 - f_f, vv, jnp.log(f_f), S0[:, 0])
    o_b, S_b = hgrn2_scan(q[:, ::-1], (1.0 - f_b)[:, ::-1], vv[:, ::-1], jnp.log(f_b)[:, ::-1], S0[:, 1])
    o = o_f + o_b[:, ::-1]
    y_hg = rms_norm(o, lp['hg_norm_g'].reshape(HG_H, HG_DV)).reshape(B, T, HG_W) * jax.nn.silu(hg_gate.astype(f32))

    qa = rms_norm(a_q.reshape(B, T, ATT_H, HEAD_DIM), lp['att_qn_g'])
    ka = rms_norm(a_k.reshape(B, T, ATT_KV, HEAD_DIM), lp['att_kn_g'])
    va = a_v.reshape(B, T, ATT_KV, HEAD_DIM)
    sink = lp['att_sink'].astype(f32).reshape(ATT_KV, ATT_G)
    if ctx is None:
        y_att = attn_context(qa, ka, va, sink)
    else:
        y_att = attn_latent(axial_rope(qa), axial_rope(ka), va, ctx[0].astype(ka.dtype), ctx[1].astype(va.dtype), sink)

    y = jnp.concatenate([y_hy, y_rg, y_hg, y_att.astype(f32)], axis=-1).astype(u.dtype) @ lp['w_out']
    new_ctx = (ka, va, jnp.stack([hT_f, hT_b], axis=1), jnp.stack([S_f, S_b], axis=1)) if ctx is None else None
    return y, new_ctx


def conv_ffn(u, lp):
    h = dwconv(u @ lp['ffn_up'], lp['ffn_conv_w'], lp['ffn_conv_b'], (FFN_CONV // 2, FFN_CONV // 2))
    val, gate = jnp.split(h, 2, axis=-1)
    return (jax.nn.silu(gate) * val) @ lp['ffn_down']


def trunk_layer(x, mod, lp, lb, ctx):
    sh1, sc1, g1, sh2, sc2, g2 = jnp.split(mod, 6, axis=-1)
    u = rms_norm(x, lp['norm1_g']) * (1.0 + sc1) + sh1
    m, new_ctx = token_mixers(u, lp, lb, ctx)
    x = x + g1 * m
    u = rms_norm(x, lp['norm2_g']) * (1.0 + sc2) + sh2
    x = x + g2 * conv_ffn(u, lp)
    return x, new_ctx


def setup_inputs(seed: int = 0) -> dict:
    key = jax.random.key(seed)
    ks = iter(jax.random.split(key, 48))
    f32 = jnp.float32

    def nrm(shape, s):
        return s * jax.random.normal(next(ks), shape, f32)

    D = D_MODEL
    inp = {}
    inp['x_prompt'] = nrm((BATCH, SEQ, D), 1.0)
    inp['x_sample'] = nrm((DEC_BATCH, DEC_SEQ, D), 1.0)
    inp['cache_k'] = nrm((DEC_BATCH, DEPTH, PAST_LEN, ATT_KV, HEAD_DIM), 1.0)
    inp['cache_v'] = nrm((DEC_BATCH, DEPTH, PAST_LEN, ATT_KV, HEAD_DIM), 1.0)
    inp['state_rglru'] = nrm((DEC_BATCH, DEPTH, 2, RG_W), 0.5)
    inp['state_hgrn'] = nrm((DEC_BATCH, DEPTH, 2, HG_H, HG_DK, HG_DV), 0.5)
    inp['c'] = nrm((DEC_BATCH, D), 1.0)
    inp['c_ctx'] = nrm((D,), 1.0)
    inp['norm1_g'] = 1.0 + nrm((DEPTH, D), 0.02)
    inp['norm2_g'] = 1.0 + nrm((DEPTH, D), 0.02)
    inp['ada_w'] = nrm((DEPTH, D, 6 * D), 0.5 * D ** -0.5)
    inp['ada_b'] = nrm((DEPTH, 6 * D), 0.02)
    inp['w_in'] = nrm((DEPTH, D, IN_COLS), D ** -0.5)
    inp['w_out'] = nrm((DEPTH, MIX_W, D), MIX_W ** -0.5)
    inp['hy_conv_w'] = nrm((DEPTH, HY_SHORT, 3 * HY_W), HY_SHORT ** -0.5)
    inp['hy_conv_b'] = nrm((DEPTH, 3 * HY_W), 0.02)
    inp['hy_f_w1'] = nrm((DEPTH, HY_EMB, HY_FH), HY_EMB ** -0.5)
    inp['hy_f_b1'] = nrm((DEPTH, HY_FH), 0.1)
    inp['hy_f_freq'] = 1.0 + nrm((DEPTH, 2, HY_FH), 0.1)
    inp['hy_f_w2'] = nrm((DEPTH, HY_FH, HY_FH), HY_FH ** -0.5)
    inp['hy_f_b2'] = nrm((DEPTH, HY_FH), 0.1)
    inp['hy_f_w3'] = nrm((DEPTH, HY_FH, 2 * HY_W), HY_FH ** -0.5)
    inp['hy_f_b3'] = nrm((DEPTH, 2 * HY_W), 0.02)
    inp['hy_decay'] = jax.random.uniform(next(ks), (DEPTH, 2 * HY_W), f32, HY_MIN_DECAY, HY_MAX_DECAY)
    inp['hy_skip'] = nrm((DEPTH, HY_W), 1.0)
    inp['rg_conv_w'] = nrm((DEPTH, RG_CONV, RG_W), RG_CONV ** -0.5)
    inp['rg_conv_b'] = nrm((DEPTH, RG_W), 0.02)
    inp['rg_gate_w'] = nrm((DEPTH, 2, 2, RG_H, RG_DH, RG_DH), RG_DH ** -0.5)
    inp['rg_gate_b'] = nrm((DEPTH, 2, 2, RG_W), 0.02)
    a0 = jax.random.uniform(next(ks), (DEPTH, 2, RG_W), f32, 0.9, 0.999)
    p = a0 ** (1.0 / RG_C)
    inp['rg_lambda'] = jnp.log(p) - jnp.log1p(-p)
    inp['hg_lb'] = nrm((DEPTH, 2, HG_W), 0.5)
    inp['hg_norm_g'] = 1.0 + nrm((DEPTH, HG_W), 0.02)
    inp['att_qn_g'] = 1.0 + nrm((DEPTH, HEAD_DIM), 0.02)
    inp['att_kn_g'] = 1.0 + nrm((DEPTH, HEAD_DIM), 0.02)
    inp['att_sink'] = nrm((DEPTH, ATT_H), 0.5)
    inp['ffn_up'] = nrm((DEPTH, D, 2 * D_FF), D ** -0.5)
    inp['ffn_conv_w'] = nrm((DEPTH, FFN_CONV, 2 * D_FF), FFN_CONV ** -0.5)
    inp['ffn_conv_b'] = nrm((DEPTH, 2 * D_FF), 0.02)
    inp['ffn_down'] = nrm((DEPTH, D_FF, D), D_FF ** -0.5)
    return inp


def reference(x_prompt, x_sample, cache_k, cache_v, state_rglru, state_hgrn, c, c_ctx,
              norm1_g, norm2_g, ada_w, ada_b, w_in, w_out,
              hy_conv_w, hy_conv_b, hy_f_w1, hy_f_b1, hy_f_freq, hy_f_w2, hy_f_b2, hy_f_w3, hy_f_b3, hy_decay, hy_skip,
              rg_conv_w, rg_conv_b, rg_gate_w, rg_gate_b, rg_lambda,
              hg_lb, hg_norm_g, att_qn_g, att_kn_g, att_sink,
              ffn_up, ffn_conv_w, ffn_conv_b, ffn_down):
    stacked = dict(norm1_g=norm1_g, norm2_g=norm2_g, ada_w=ada_w, ada_b=ada_b, w_in=w_in, w_out=w_out,
                   hy_conv_w=hy_conv_w, hy_conv_b=hy_conv_b, hy_f_w1=hy_f_w1, hy_f_b1=hy_f_b1, hy_f_freq=hy_f_freq,
                   hy_f_w2=hy_f_w2, hy_f_b2=hy_f_b2, hy_f_w3=hy_f_w3, hy_f_b3=hy_f_b3, hy_decay=hy_decay, hy_skip=hy_skip,
                   rg_conv_w=rg_conv_w, rg_conv_b=rg_conv_b, rg_gate_w=rg_gate_w, rg_gate_b=rg_gate_b, rg_lambda=rg_lambda,
                   hg_norm_g=hg_norm_g, att_qn_g=att_qn_g, att_kn_g=att_kn_g, att_sink=att_sink,
                   ffn_up=ffn_up, ffn_conv_w=ffn_conv_w, ffn_conv_b=ffn_conv_b, ffn_down=ffn_down)
    lbs = jnp.cumsum(jax.nn.softmax(hg_lb.astype(jnp.float32), axis=0), axis=0)
    lbs = lbs - lbs[0:1]
    y_prompt, y_sample = x_prompt, x_sample
    ks, vs, rgs, hgs = [], [], [], []
    for l in range(DEPTH):
        lp = {name: arr[l] for name, arr in stacked.items()}
        mod_p = (jax.nn.silu(c_ctx) @ lp['ada_w'] + lp['ada_b'])[None, None, :]
        y_prompt, ctx_new = trunk_layer(y_prompt, mod_p, lp, lbs[l], None)
        ks.append(ctx_new[0])
        vs.append(ctx_new[1])
        rgs.append(ctx_new[2])
        hgs.append(ctx_new[3])
        mod_s = (jax.nn.silu(c) @ lp['ada_w'] + lp['ada_b'])[:, None, :]
        cached = (cache_k[:, l], cache_v[:, l], state_rglru[:, l], state_hgrn[:, l])
        y_sample, _ = trunk_layer(y_sample, mod_s, lp, lbs[l], cached)
    new_cache_k = jnp.stack(ks, axis=1)
    new_cache_v = jnp.stack(vs, axis=1)
    new_state_rglru = jnp.stack(rgs, axis=1)
    new_state_hgrn = jnp.stack(hgs, axis=1)
    return (y_prompt, y_sample, new_cache_k, new_cache_v, new_state_rglru, new_state_hgrn)
```

```python
import functools
import math

import jax
import jax.numpy as jnp
import numpy as np
from jax import lax
from jax.experimental import pallas as pl
from jax.experimental.pallas import tpu as pltpu

D_MODEL = 2048
DEPTH = 2
GRID_W = 64
GROUP_W = D_MODEL // 4
MIX_W = 4 * GROUP_W
HY_W = GROUP_W
HY_BANDS = 16
HY_SHORT = 3
RG_W = GROUP_W
RG_H = 4
RG_DH = RG_W // RG_H
RG_CONV = 4
RG_PAD = ((RG_CONV - 1) // 2, RG_CONV // 2)
RG_C = 8.0
HG_W = GROUP_W
HG_H = 4
HG_DK = HG_W // HG_H
HG_DV = HG_W // HG_H
HG_CHUNK = 32
ATT_W = GROUP_W
HEAD_DIM = 64
ATT_H = ATT_W // HEAD_DIM
ATT_KV = 2
ATT_G = ATT_H // ATT_KV
WINDOW = 128
ATT_BLOCK = 128
ATT_SCALE = 1.0 / math.sqrt(HEAD_DIM)
ROPE_BASE = 10000.0
NEG_INF = -1e30
D_FF = 5632
FFN_CONV = 3
IN_SIZES = (3 * HY_W, RG_W, RG_W, HG_W, HG_W, HG_W, HG_W, HG_W, ATT_W, ATT_KV * HEAD_DIM, ATT_KV * HEAD_DIM)
IN_COLS = sum(IN_SIZES)
NORM_EPS = 1e-6

V7X_VMEM_LIMIT_BYTES = 56 * 1024 * 1024
HALO_ROWS = 16
IN_COLS_PAD = 6144
IN_TILE_N = 512
FFN_TILE_F = 512
ROW_TILE = 512


def _compiler_params(semantics):
    return pltpu.CompilerParams(dimension_semantics=semantics, vmem_limit_bytes=V7X_VMEM_LIMIT_BYTES)


def _in_proj_kernel(x_ref, gs_ref, sh_ref, w_ref, z_ref, u_ref):
    @pl.when(pl.program_id(1) == 0)
    def _():
        x = x_ref[...]
        inv = lax.rsqrt(jnp.mean(x * x, axis=-1, keepdims=True) + NORM_EPS)
        u_ref[...] = (x * inv * gs_ref[0] + sh_ref[0]).astype(jnp.bfloat16)

    z_ref[...] = jnp.dot(u_ref[...], w_ref[...], preferred_element_type=jnp.float32)


def in_proj(x2d, gs, sh, w_bf16, seq_len, tm=ROW_TILE, tn=IN_TILE_N):
    M, D = x2d.shape
    Np = w_bf16.shape[1]
    bm = gs.shape[0]
    mod_idx = (lambda i, j: (i * tm // seq_len, 0, 0)) if bm > 1 else (lambda i, j: (0, 0, 0))
    return pl.pallas_call(
        _in_proj_kernel,
        grid=(M // tm, Np // tn),
        in_specs=[
            pl.BlockSpec((tm, D), lambda i, j: (i, 0)),
            pl.BlockSpec((1, 1, D), mod_idx),
            pl.BlockSpec((1, 1, D), mod_idx),
            pl.BlockSpec((D, tn), lambda i, j: (0, j)),
        ],
        out_specs=pl.BlockSpec((tm, tn), lambda i, j: (i, j)),
        out_shape=jax.ShapeDtypeStruct((M, Np), jnp.float32),
        scratch_shapes=[pltpu.VMEM((tm, D), jnp.bfloat16)],
        compiler_params=_compiler_params(("parallel", "arbitrary")),
        name="in_proj",
    )(x2d, gs, sh, w_bf16)


def _out_proj_kernel(y_ref, w_ref, x_ref, g1_ref, gs2_ref, sh2_ref, x1_ref, u2_ref):
    m = jnp.dot(y_ref[...], w_ref[...], preferred_element_type=jnp.float32)
    x1 = x_ref[...] + g1_ref[0] * m
    x1_ref[...] = x1
    inv = lax.rsqrt(jnp.mean(x1 * x1, axis=-1, keepdims=True) + NORM_EPS)
    u2_ref[...] = (x1 * inv * gs2_ref[0] + sh2_ref[0]).astype(jnp.bfloat16)


def out_proj(y_bf16, w_bf16, x2d, g1, gs2, sh2, seq_len, tm=ROW_TILE):
    M, D = x2d.shape
    K = y_bf16.shape[1]
    bm = g1.shape[0]
    mod_idx = (lambda i: (i * tm // seq_len, 0, 0)) if bm > 1 else (lambda i: (0, 0, 0))
    mod_spec = pl.BlockSpec((1, 1, D), mod_idx)
    return pl.pallas_call(
        _out_proj_kernel,
        grid=(M // tm,),
        in_specs=[
            pl.BlockSpec((tm, K), lambda i: (i, 0)),
            pl.BlockSpec((K, D), lambda i: (0, 0)),
            pl.BlockSpec((tm, D), lambda i: (i, 0)),
            mod_spec, mod_spec, mod_spec,
        ],
        out_specs=[pl.BlockSpec((tm, D), lambda i: (i, 0)), pl.BlockSpec((tm, D), lambda i: (i, 0))],
        out_shape=[jax.ShapeDtypeStruct((M, D), jnp.float32), jax.ShapeDtypeStruct((M, D), jnp.bfloat16)],
        compiler_params=_compiler_params(("parallel",)),
        name="out_proj",
    )(y_bf16, w_bf16, x2d, g1, gs2, sh2)


def _conv_ffn_kernel(u_ref, up_ref, un_ref, wv_ref, wg_ref, cwv_ref, cwg_ref, cbv_ref, cbg_ref,
                     dn_ref, x1_ref, g2_ref, o_ref, *, tm, seq_len):
    i = pl.program_id(0)
    j = pl.program_id(1)
    u = u_ref[...]
    u_prev = up_ref[...]
    u_next = un_ref[...]
    pos = (lax.broadcasted_iota(jnp.int32, (tm, 1), 0) + i * tm) % seq_len
    row = lax.broadcasted_iota(jnp.int32, (tm, 1), 0)
    has_prev = pos != 0
    has_next = pos != seq_len - 1

    def branch(w_ref, cw_ref, cb_ref):
        w = w_ref[...]
        h = jnp.dot(u, w, preferred_element_type=jnp.float32)
        h_before = jnp.dot(u_prev, w, preferred_element_type=jnp.float32)[HALO_ROWS - 1:HALO_ROWS]
        h_after = jnp.dot(u_next, w, preferred_element_type=jnp.float32)[0:1]
        h_m1 = jnp.where(row == 0, h_before, pltpu.roll(h, 1, axis=0))
        h_p1 = jnp.where(row == tm - 1, h_after, pltpu.roll(h, tm - 1, axis=0))
        h_m1 = jnp.where(has_prev, h_m1, 0.0)
        h_p1 = jnp.where(has_next, h_p1, 0.0)
        cw = cw_ref[...]
        return cw[0:1] * h_m1 + cw[1:2] * h + cw[2:3] * h_p1 + cb_ref[...]

    val = branch(wv_ref, cwv_ref, cbv_ref)
    gate = branch(wg_ref, cwg_ref, cbg_ref)
    a = (gate * jax.nn.sigmoid(gate) * val).astype(jnp.bfloat16)
    part = jnp.dot(a, dn_ref[...], preferred_element_type=jnp.float32)

    @pl.when(j == 0)
    def _():
        o_ref[...] = part

    @pl.when(j > 0)
    def _():
        o_ref[...] += part

    @pl.when(j == pl.num_programs(1) - 1)
    def _():
        o_ref[...] = x1_ref[...] + g2_ref[0] * o_ref[...]


def conv_ffn(u2, up_bf16, conv_w, conv_b, down_bf16, x1, g2, seq_len, tm=ROW_TILE, tf=FFN_TILE_F):
    M, D = x1.shape
    F = down_bf16.shape[0]
    nf = F // tf
    bm = g2.shape[0]
    hb = tm // HALO_ROWS
    n_halo = M // HALO_ROWS
    mod_idx = (lambda i, j: (i * tm // seq_len, 0, 0)) if bm > 1 else (lambda i, j: (0, 0, 0))
    conv_b2 = conv_b.reshape(1, 2 * F)
    kern = functools.partial(_conv_ffn_kernel, tm=tm, seq_len=seq_len)
    return pl.pallas_call(
        kern,
        grid=(M // tm, nf),
        in_specs=[
            pl.BlockSpec((tm, D), lambda i, j: (i, 0)),
            pl.BlockSpec((HALO_ROWS, D), lambda i, j: (jnp.maximum(i * hb - 1, 0), 0)),
            pl.BlockSpec((HALO_ROWS, D), lambda i, j: (jnp.minimum((i + 1) * hb, n_halo - 1), 0)),
            pl.BlockSpec((D, tf), lambda i, j: (0, j)),
            pl.BlockSpec((D, tf), lambda i, j: (0, j + nf)),
            pl.BlockSpec((FFN_CONV, tf), lambda i, j: (0, j)),
            pl.BlockSpec((FFN_CONV, tf), lambda i, j: (0, j + nf)),
            pl.BlockSpec((1, tf), lambda i, j: (0, j)),
            pl.BlockSpec((1, tf), lambda i, j: (0, j + nf)),
            pl.BlockSpec((tf, D), lambda i, j: (j, 0)),
            pl.BlockSpec((tm, D), lambda i, j: (i, 0)),
            pl.BlockSpec((1, 1, D), mod_idx),
        ],
        out_specs=pl.BlockSpec((tm, D), lambda i, j: (i, 0)),
        out_shape=jax.ShapeDtypeStruct((M, D), jnp.float32),
        compiler_params=_compiler_params(("parallel", "arbitrary")),
        name="conv_ffn",
    )(u2, u2, u2, up_bf16, up_bf16, conv_w, conv_w, conv_b2, conv_b2, down_bf16, x1, g2)


def _segment_scan(a, b, reverse):
    n = a.shape[0]
    row = lax.broadcasted_iota(jnp.int32, (n, 1), 0)
    s = 1
    while s < n:
        shift = n - s if reverse else s
        valid = (row < n - s) if reverse else (row >= s)
        a_sh = jnp.where(valid, pltpu.roll(a, shift, axis=0), 1.0)
        b_sh = jnp.where(valid, pltpu.roll(b, shift, axis=0), 0.0)
        b = b + a * b_sh
        a = a * a_sh
        s *= 2
    return a, b


def _rglru_kernel(x_ref, g_ref, cw_ref, cb_ref, w_ref, gb_ref, ca_ref, h0_ref, y_ref, hT_ref, rx_s, hf_s,
                  *, seq_len, chunk):
    n_chunks = seq_len // chunk
    row = lax.broadcasted_iota(jnp.int32, (chunk, 1), 0)
    cw = cw_ref[...]
    cb = cb_ref[...]

    def conv_body(k, carry):
        c0 = pl.multiple_of(k * chunk, chunk)
        x = x_ref[pl.ds(c0, chunk), :]
        before = x_ref[pl.ds(pl.multiple_of(jnp.maximum(c0 - 8, 0), 8), 8), :]
        after = x_ref[pl.ds(pl.multiple_of(jnp.minimum(c0 + chunk, seq_len - 8), 8), 8), :]
        pos = row + c0
        x_m1 = jnp.where(row == 0, before[7:8], pltpu.roll(x, 1, axis=0))
        x_p1 = jnp.where(row == chunk - 1, after[0:1], pltpu.roll(x, chunk - 1, axis=0))
        x_p2 = jnp.where(row == chunk - 2, after[0:1],
                         jnp.where(row == chunk - 1, after[1:2], pltpu.roll(x, chunk - 2, axis=0)))
        x_m1 = jnp.where(pos >= 1, x_m1, 0.0)
        x_p1 = jnp.where(pos < seq_len - 1, x_p1, 0.0)
        x_p2 = jnp.where(pos < seq_len - 2, x_p2, 0.0)
        rx_s[pl.ds(c0, chunk), :] = cw[0:1] * x_m1 + cw[1:2] * x + cw[2:3] * x_p1 + cw[3:4] * x_p2 + cb
        return carry

    lax.fori_loop(0, n_chunks, conv_body, 0)

    def affine(c0, d):
        rx = rx_s[pl.ds(c0, chunk), :]
        w = w_ref[0][:, d * 2 * RG_DH:(d + 1) * 2 * RG_DH]
        gates = jnp.dot(rx.astype(jnp.bfloat16), w, preferred_element_type=jnp.float32)
        gates = gates + gb_ref[0][:, d * 2 * RG_DH:(d + 1) * 2 * RG_DH]
        r = jax.nn.sigmoid(gates[:, :RG_DH])
        i = jax.nn.sigmoid(gates[:, RG_DH:])
        a = jnp.exp(ca_ref[d:d + 1, :] * r)
        b = jnp.sqrt(1.0 - a * a) * (i * rx)
        return a, b

    def fwd_body(k, h):
        c0 = pl.multiple_of(k * chunk, chunk)
        a, b = affine(c0, 0)
        a, b = _segment_scan(a, b, reverse=False)
        hh = b + a * h
        hf_s[pl.ds(c0, chunk), :] = hh
        return hh[chunk - 1:chunk]

    h_f = lax.fori_loop(0, n_chunks, fwd_body, h0_ref[0][0:1, :])

    def bwd_body(kk, h):
        c0 = pl.multiple_of((n_chunks - 1 - kk) * chunk, chunk)
        a, b = affine(c0, 1)
        a, b = _segment_scan(a, b, reverse=True)
        hh = b + a * h
        y_ref[pl.ds(c0, chunk), :] = (hf_s[pl.ds(c0, chunk), :] + hh) * jax.nn.gelu(g_ref[pl.ds(c0, chunk), :])
        return hh[0:1]

    h_b = lax.fori_loop(0, n_chunks, bwd_body, h0_ref[0][1:2, :])
    hT_ref[0] = jnp.concatenate([h_f, h_b], axis=0)


RG_X_COL = 3 * HY_W
RG_GATE_COL = RG_X_COL + RG_W


def rglru(z2d, lp, h0, seq_len, chunk=128):
    M = z2d.shape[0]
    B = M // seq_len
    f32 = jnp.float32
    gw = lp['rg_gate_w']
    w = jnp.transpose(gw, (2, 3, 0, 1, 4)).reshape(RG_H, RG_DH, 4 * RG_DH).astype(jnp.bfloat16)
    gb = jnp.transpose(lp['rg_gate_b'].reshape(2, 2, RG_H, RG_DH), (2, 0, 1, 3)).reshape(RG_H, 1, 4 * RG_DH)
    ca = -RG_C * jax.nn.softplus(-lp['rg_lambda'].astype(f32))
    xcol = RG_X_COL // RG_DH
    gcol = RG_GATE_COL // RG_DH
    kern = functools.partial(_rglru_kernel, seq_len=seq_len, chunk=chunk)
    return pl.pallas_call(
        kern,
        grid=(B, RG_H),
        in_specs=[
            pl.BlockSpec((seq_len, RG_DH), lambda b, h: (b, xcol + h)),
            pl.BlockSpec((seq_len, RG_DH), lambda b, h: (b, gcol + h)),
            pl.BlockSpec((RG_CONV, RG_DH), lambda b, h: (0, h)),
            pl.BlockSpec((1, RG_DH), lambda b, h: (0, h)),
            pl.BlockSpec((1, RG_DH, 4 * RG_DH), lambda b, h: (h, 0, 0)),
            pl.BlockSpec((1, 1, 4 * RG_DH), lambda b, h: (h, 0, 0)),
            pl.BlockSpec((2, RG_DH), lambda b, h: (0, h)),
            pl.BlockSpec((1, 2, RG_DH), lambda b, h: (b, 0, h)),
        ],
        out_specs=[
            pl.BlockSpec((seq_len, RG_DH), lambda b, h: (b, h)),
            pl.BlockSpec((1, 2, RG_DH), lambda b, h: (b, 0, h)),
        ],
        out_shape=[jax.ShapeDtypeStruct((M, RG_W), f32), jax.ShapeDtypeStruct((B, 2, RG_W), f32)],
        scratch_shapes=[pltpu.VMEM((seq_len, RG_DH), f32), pltpu.VMEM((seq_len, RG_DH), f32)],
        compiler_params=_compiler_params(("parallel", "parallel")),
        name="rglru",
    )(z2d, z2d, lp['rg_conv_w'], lp['rg_conv_b'].reshape(1, RG_W), w, gb, ca, h0.astype(f32))


def rms_norm(x, g, eps=NORM_EPS):
    xf = x.astype(jnp.float32)
    y = xf * lax.rsqrt(jnp.mean(xf * xf, axis=-1, keepdims=True) + eps)
    return (y * g.astype(jnp.float32)).astype(x.dtype)


def dwconv(x, w, b, pad):
    y = lax.conv_general_dilated(x, w.astype(x.dtype)[:, None, :], window_strides=(1,), padding=[pad],
                                 dimension_numbers=('NWC', 'WIO', 'NWC'), feature_group_count=x.shape[-1])
    return y + b.astype(x.dtype)


def hyena_kernel(L, w1, b1, freq, w2, b2, w3, b3, decay):
    f32 = jnp.float32
    t_idx = jnp.arange(L, dtype=f32)
    t = t_idx / (L - 1)
    ang = (2.0 * math.pi * t_idx / L)[:, None] * jnp.linspace(1e-4, HY_BANDS - 1, HY_BANDS, dtype=f32)[None, :]
    z = jnp.concatenate([t[:, None], jnp.cos(ang), -jnp.sin(ang)], axis=-1)
    h = jnp.sin(freq[0].astype(f32) * (z @ w1.astype(f32) + b1.astype(f32)))
    h = jnp.sin(freq[1].astype(f32) * (h @ w2.astype(f32) + b2.astype(f32)))
    h = (h @ w3.astype(f32) + b3.astype(f32)) * jnp.exp(-t[:, None] * jnp.abs(decay.astype(f32)))
    h_fwd, h_bwd = h[:, :HY_W], h[:, HY_W:]
    k = jnp.concatenate([h_fwd, jnp.zeros((1, HY_W), f32), h_bwd[:0:-1]], axis=0)
    return k / jnp.sum(jnp.abs(k), axis=0, keepdims=True)


def centred_long_conv(u, k, skip):
    L = u.shape[1]
    spec = jnp.fft.rfft(u, n=2 * L, axis=1) * jnp.fft.rfft(k, axis=0)[None]
    return jnp.fft.irfft(spec, n=2 * L, axis=1)[:, :L] + u * skip


def _affine_combine(e1, e2):
    a1, b1 = e1
    a2, b2 = e2
    return a1 * a2, a2 * b1 + b2


def rglru_scan(x, gate_w, gate_b, lam, h0):
    f32 = jnp.float32
    B, T, W = x.shape
    xh = x.reshape(B, T, RG_H, RG_DH)
    gates = jnp.einsum('bthi,ghij->gbthj', xh, gate_w.astype(f32)).reshape(2, B, T, W) + gate_b.astype(f32)[:, None, None, :]
    r = jax.nn.sigmoid(gates[0])
    i = jax.nn.sigmoid(gates[1])
    log_a = -RG_C * r * jax.nn.softplus(-lam.astype(f32))
    a = jnp.exp(log_a)
    b = jnp.sqrt(-jnp.expm1(2.0 * log_a)) * (i * x)
    b = b.at[:, 0].add(a[:, 0] * h0)
    _, h = lax.associative_scan(_affine_combine, (a, b), axis=1)
    return h, h[:, -1]


def hgrn2_scan(q, k, v, log_f, S0):
    B, T, H, Dk = q.shape
    Dv = v.shape[-1]
    N = T // HG_CHUNK

    def chunks(a):
        return a.reshape(B, N, HG_CHUNK, H, a.shape[-1]).transpose(1, 0, 3, 2, 4)

    causal = jnp.tril(jnp.ones((HG_CHUNK, HG_CHUNK), jnp.float32))
    mid = HG_CHUNK // 2 - 1

    def step(S, inp):
        qc, kc, vc, gc = inp
        bc = jnp.cumsum(gc, axis=2)
        ref = bc[:, :, mid:mid + 1]
        att = jnp.einsum('bhtd,bhsd->bhts', qc * jnp.exp(bc - ref), kc * jnp.exp(ref - bc)) * causal
        o = jnp.einsum('bhts,bhse->bhte', att, vc) + jnp.einsum('bhtd,bhde->bhte', qc * jnp.exp(bc), S)
        bl = bc[:, :, -1:]
        S = jnp.exp(bl)[:, :, 0, :, None] * S + jnp.einsum('bhsd,bhse->bhde', kc * jnp.exp(bl - bc), vc)
        return S, o

    S, o = lax.scan(step, S0, (chunks(q), chunks(k), chunks(v), chunks(log_f)))
    return o.transpose(1, 0, 3, 2, 4).reshape(B, T, H, Dv), S


def axial_rope(x):
    f32 = jnp.float32
    T = x.shape[1]
    rows = T // GRID_W
    row = jnp.repeat(jnp.arange(rows, dtype=f32), GRID_W)
    col = jnp.tile(jnp.arange(GRID_W, dtype=f32), rows)
    half = HEAD_DIM // 2
    inv = ROPE_BASE ** (-jnp.arange(0, half, 2, dtype=f32) / half)

    def rot(xa, pos):
        ang = pos[:, None] * inv[None, :]
        cos = jnp.cos(ang)[None, :, None, :]
        sin = jnp.sin(ang)[None, :, None, :]
        x1, x2 = jnp.split(xa.astype(f32), 2, axis=-1)
        return jnp.concatenate([x1 * cos - x2 * sin, x1 * sin + x2 * cos], axis=-1)

    return jnp.concatenate([rot(x[..., :half], row), rot(x[..., half:], col)], axis=-1).astype(x.dtype)


def sink_attend(qb, keys, vals, bias, sink):
    s = jnp.einsum('bqkgd,bskd->bkgqs', qb, keys).astype(jnp.float32) * ATT_SCALE + bias
    sk = sink[None, :, :, None, None]
    m = jnp.maximum(jnp.max(s, axis=-1, keepdims=True), sk)
    e = jnp.exp(s - m)
    p = e / (jnp.sum(e, axis=-1, keepdims=True) + jnp.exp(sk - m))
    return jnp.einsum('bkgqs,bskd->bqkgd', p.astype(vals.dtype), vals)


def attn_context(q, k, v, sink):
    B, S = q.shape[:2]
    nb = S // ATT_BLOCK
    qb = q.reshape(B, nb, ATT_BLOCK, ATT_KV, ATT_G, HEAD_DIM).swapaxes(0, 1)
    o = lax.map(lambda qq: sink_attend(qq, k, v, 0.0, sink), qb)
    return o.swapaxes(0, 1).reshape(B, S, ATT_W)


def attn_latent(q, k, v, kc, vc, sink):
    B, T = q.shape[:2]
    nb = T // ATT_BLOCK
    P = kc.shape[1]

    def blocks3(a):
        ap = jnp.pad(a, ((0, 0), (ATT_BLOCK, ATT_BLOCK), (0, 0), (0, 0))).reshape(B, nb + 2, ATT_BLOCK, ATT_KV, HEAD_DIM)
        return jnp.concatenate([ap[:, :-2], ap[:, 1:-1], ap[:, 2:]], axis=2).swapaxes(0, 1)

    kl, vl = blocks3(k), blocks3(v)
    qb = q.reshape(B, nb, ATT_BLOCK, ATT_KV, ATT_G, HEAD_DIM).swapaxes(0, 1)
    base = jnp.arange(nb)[:, None, None] * ATT_BLOCK
    qpos = base + jnp.arange(ATT_BLOCK)[None, :, None]
    kpos = base - ATT_BLOCK + jnp.arange(3 * ATT_BLOCK)[None, None, :]
    ok = (jnp.abs(qpos - kpos) <= WINDOW) & (kpos >= 0) & (kpos < T)
    bias = jnp.concatenate([jnp.where(ok, 0.0, NEG_INF).astype(jnp.float32),
                            jnp.zeros((nb, ATT_BLOCK, P), jnp.float32)], axis=-1)

    def one(args):
        qq, kk, vv, bb = args
        return sink_attend(qq, jnp.concatenate([kk, kc], axis=1), jnp.concatenate([vv, vc], axis=1), bb, sink)

    o = lax.map(one, (qb, kl, vl, bias))
    return o.swapaxes(0, 1).reshape(B, T, ATT_W)


def token_mixers(z, lp, lb, ctx):
    f32 = jnp.float32
    B, T, _ = z.shape
    (hy_in, rg_x, rg_gate, hg_q, hg_i, hg_ff, hg_fb, hg_gate, a_q, a_k, a_v) = jnp.split(
        z[..., :IN_COLS], np.cumsum(IN_SIZES)[:-1].tolist(), axis=-1)

    hy = dwconv(hy_in, lp['hy_conv_w'], lp['hy_conv_b'], (HY_SHORT // 2, HY_SHORT // 2)).astype(f32)
    hv, hx1, hx2 = jnp.split(hy, 3, axis=-1)
    k_long = hyena_kernel(T, lp['hy_f_w1'], lp['hy_f_b1'], lp['hy_f_freq'], lp['hy_f_w2'], lp['hy_f_b2'],
                          lp['hy_f_w3'], lp['hy_f_b3'], lp['hy_decay'])
    y_hy = hx1 * centred_long_conv(hv * hx2, k_long, lp['hy_skip'].astype(f32))

    h0 = jnp.zeros((B, 2, RG_W), f32) if ctx is None else ctx[2]
    y_rg, hT_rg = rglru(z.reshape(B * T, -1), lp, h0, T)
    y_rg = y_rg.reshape(B, T, RG_W)

    S0 = jnp.zeros((B, 2, HG_H, HG_DK, HG_DV), f32) if ctx is None else ctx[3].astype(f32)
    q = hg_q.astype(f32).reshape(B, T, HG_H, HG_DK)
    vv = hg_i.astype(f32).reshape(B, T, HG_H, HG_DV)
    f_f = (lb[0] + (1.0 - lb[0]) * jax.nn.sigmoid(hg_ff.astype(f32))).reshape(B, T, HG_H, HG_DK)
    f_b = (lb[1] + (1.0 - lb[1]) * jax.nn.sigmoid(hg_fb.astype(f32))).reshape(B, T, HG_H, HG_DK)
    o_f, S_f = hgrn2_scan(q, 1.0 - f_f, vv, jnp.log(f_f), S0[:, 0])
    o_b, S_b = hgrn2_scan(q[:, ::-1], (1.0 - f_b)[:, ::-1], vv[:, ::-1], jnp.log(f_b)[:, ::-1], S0[:, 1])
    o = o_f + o_b[:, ::-1]
    y_hg = rms_norm(o, lp['hg_norm_g'].reshape(HG_H, HG_DV)).reshape(B, T, HG_W) * jax.nn.silu(hg_gate.astype(f32))

    qa = rms_norm(a_q.reshape(B, T, ATT_H, HEAD_DIM), lp['att_qn_g'])
    ka = rms_norm(a_k.reshape(B, T, ATT_KV, HEAD_DIM), lp['att_kn_g'])
    va = a_v.reshape(B, T, ATT_KV, HEAD_DIM)
    sink = lp['att_sink'].astype(f32).reshape(ATT_KV, ATT_G)
    if ctx is None:
        y_att = attn_context(qa, ka, va, sink)
    else:
        y_att = attn_latent(axial_rope(qa), axial_rope(ka), va, ctx[0].astype(ka.dtype), ctx[1].astype(va.dtype), sink)

    y = jnp.concatenate([y_hy, y_rg, y_hg, y_att.astype(f32)], axis=-1).astype(jnp.bfloat16)
    new_ctx = (ka, va, hT_rg, jnp.stack([S_f, S_b], axis=1)) if ctx is None else None
    return y, new_ctx


def trunk_layer(x, mod, lp, lb, ctx):
    B, T, D = x.shape
    sh1, sc1, g1, sh2, sc2, g2 = [m[:, None, :] for m in jnp.split(mod, 6, axis=-1)]
    gs1 = lp['norm1_g'][None, None, :] * (1.0 + sc1)
    gs2 = lp['norm2_g'][None, None, :] * (1.0 + sc2)
    x2d = x.reshape(B * T, D)
    z = in_proj(x2d, gs1, sh1, lp['w_in_bf16'], T)
    y, new_ctx = token_mixers(z.reshape(B, T, IN_COLS_PAD), lp, lb, ctx)
    x1, u2 = out_proj(y.reshape(B * T, MIX_W), lp['w_out_bf16'], x2d, g1, gs2, sh2, T)
    x2 = conv_ffn(u2, lp['ffn_up_bf16'], lp['ffn_conv_w'], lp['ffn_conv_b'], lp['ffn_down_bf16'], x1, g2, T)
    return x2.reshape(B, T, D), new_ctx


def kernel(x_prompt, x_sample, cache_k, cache_v, state_rglru, state_hgrn, c, c_ctx, norm1_g, norm2_g, ada_w, ada_b, w_in, w_out, hy_conv_w, hy_conv_b, hy_f_w1, hy_f_b1, hy_f_freq, hy_f_w2, hy_f_b2, hy_f_w3, hy_f_b3, hy_decay, hy_skip, rg_conv_w, rg_conv_b, rg_gate_w, rg_gate_b, rg_lambda, hg_lb, hg_norm_g, att_qn_g, att_kn_g, att_sink, ffn_up, ffn_conv_w, ffn_conv_b, ffn_down):
    bf16 = jnp.bfloat16
    stacked = dict(norm1_g=norm1_g, norm2_g=norm2_g, ada_w=ada_w, ada_b=ada_b,
                   hy_conv_w=hy_conv_w, hy_conv_b=hy_conv_b, hy_f_w1=hy_f_w1, hy_f_b1=hy_f_b1, hy_f_freq=hy_f_freq,
                   hy_f_w2=hy_f_w2, hy_f_b2=hy_f_b2, hy_f_w3=hy_f_w3, hy_f_b3=hy_f_b3, hy_decay=hy_decay, hy_skip=hy_skip,
                   rg_conv_w=rg_conv_w, rg_conv_b=rg_conv_b, rg_gate_w=rg_gate_w, rg_gate_b=rg_gate_b, rg_lambda=rg_lambda,
                   hg_norm_g=hg_norm_g, att_qn_g=att_qn_g, att_kn_g=att_kn_g, att_sink=att_sink,
                   ffn_conv_w=ffn_conv_w, ffn_conv_b=ffn_conv_b,
                   w_in_bf16=jnp.pad(w_in.astype(bf16), ((0, 0), (0, 0), (0, IN_COLS_PAD - IN_COLS))),
                   w_out_bf16=w_out.astype(bf16), ffn_up_bf16=ffn_up.astype(bf16), ffn_down_bf16=ffn_down.astype(bf16))
    lbs = jnp.cumsum(jax.nn.softmax(hg_lb.astype(jnp.float32), axis=0), axis=0)
    lbs = lbs - lbs[0:1]
    y_prompt, y_sample = x_prompt, x_sample
    ks, vs, rgs, hgs = [], [], [], []
    for l in range(DEPTH):
        lp = {name: arr[l] for name, arr in stacked.items()}
        mod_p = (jax.nn.silu(c_ctx) @ lp['ada_w'] + lp['ada_b'])[None, :]
        y_prompt, ctx_new = trunk_layer(y_prompt, mod_p, lp, lbs[l], None)
        ks.append(ctx_new[0])
        vs.append(ctx_new[1])
        rgs.append(ctx_new[2])
        hgs.append(ctx_new[3])
        mod_s = jax.nn.silu(c) @ lp['ada_w'] + lp['ada_b']
        cached = (cache_k[:, l], cache_v[:, l], state_rglru[:, l], state_hgrn[:, l])
        y_sample, _ = trunk_layer(y_sample, mod_s, lp, lbs[l], cached)
    return (y_prompt, y_sample, jnp.stack(ks, axis=1), jnp.stack(vs, axis=1),
            jnp.stack(rgs, axis=1), jnp.stack(hgs, axis=1))
```

```python
import functools
import math

import jax
import jax.numpy as jnp
import numpy as np
from jax import lax
from jax.experimental import pallas as pl
from jax.experimental.pallas import tpu as pltpu

D_MODEL = 2048
DEPTH = 2
GRID_W = 64
GROUP_W = D_MODEL // 4
MIX_W = 4 * GROUP_W
HY_W = GROUP_W
HY_BANDS = 16
HY_SHORT = 3
RG_W = GROUP_W
RG_H = 4
RG_DH = RG_W // RG_H
RG_CONV = 4
RG_PAD = ((RG_CONV - 1) // 2, RG_CONV // 2)
RG_C = 8.0
HG_W = GROUP_W
HG_H = 4
HG_DK = HG_W // HG_H
HG_DV = HG_W // HG_H
HG_CHUNK = 32
ATT_W = GROUP_W
HEAD_DIM = 64
ATT_H = ATT_W // HEAD_DIM
ATT_KV = 2
ATT_G = ATT_H // ATT_KV
WINDOW = 128
ATT_BLOCK = 128
ATT_SCALE = 1.0 / math.sqrt(HEAD_DIM)
ROPE_BASE = 10000.0
NEG_INF = -1e30
D_FF = 5632
FFN_CONV = 3
IN_SIZES = (3 * HY_W, RG_W, RG_W, HG_W, HG_W, HG_W, HG_W, HG_W, ATT_W, ATT_KV * HEAD_DIM, ATT_KV * HEAD_DIM)
IN_COLS = sum(IN_SIZES)
NORM_EPS = 1e-6

V7X_VMEM_LIMIT_BYTES = 56 * 1024 * 1024
HALO_ROWS = 16
IN_COLS_PAD = 6144
IN_TILE_N = 512
FFN_TILE_F = 512
ROW_TILE = 512


def _compiler_params(semantics):
    return pltpu.CompilerParams(dimension_semantics=semantics, vmem_limit_bytes=V7X_VMEM_LIMIT_BYTES)


def _in_proj_kernel(x_ref, gs_ref, sh_ref, w_ref, z_ref, u_ref):
    @pl.when(pl.program_id(1) == 0)
    def _():
        x = x_ref[...]
        inv = lax.rsqrt(jnp.mean(x * x, axis=-1, keepdims=True) + NORM_EPS)
        u_ref[...] = (x * inv * gs_ref[0] + sh_ref[0]).astype(jnp.bfloat16)

    z_ref[...] = jnp.dot(u_ref[...], w_ref[...], preferred_element_type=jnp.float32)


def in_proj(x2d, gs, sh, w_bf16, seq_len, tm=ROW_TILE, tn=IN_TILE_N):
    M, D = x2d.shape
    Np = w_bf16.shape[1]
    bm = gs.shape[0]
    mod_idx = (lambda i, j: (i * tm // seq_len, 0, 0)) if bm > 1 else (lambda i, j: (0, 0, 0))
    return pl.pallas_call(
        _in_proj_kernel,
        grid=(M // tm, Np // tn),
        in_specs=[
            pl.BlockSpec((tm, D), lambda i, j: (i, 0)),
            pl.BlockSpec((1, 1, D), mod_idx),
            pl.BlockSpec((1, 1, D), mod_idx),
            pl.BlockSpec((D, tn), lambda i, j: (0, j)),
        ],
        out_specs=pl.BlockSpec((tm, tn), lambda i, j: (i, j)),
        out_shape=jax.ShapeDtypeStruct((M, Np), jnp.float32),
        scratch_shapes=[pltpu.VMEM((tm, D), jnp.bfloat16)],
        compiler_params=_compiler_params(("parallel", "arbitrary")),
        name="in_proj",
    )(x2d, gs, sh, w_bf16)


def _out_proj_kernel(conv_ref, hx1_ref, base_ref, rg_ref, hg_ref, att_ref, w_ref, x_ref, g1_ref, gs2_ref, sh2_ref,
                     x1_ref, u2_ref):
    bf16 = jnp.bfloat16
    y_hy = hx1_ref[...] * conv_ref[...] + base_ref[...]
    m = jnp.dot(y_hy.astype(bf16), w_ref[0:GROUP_W, :], preferred_element_type=jnp.float32)
    for k, ref in enumerate((rg_ref, hg_ref, att_ref), start=1):
        m += jnp.dot(ref[...].astype(bf16), w_ref[k * GROUP_W:(k + 1) * GROUP_W, :], preferred_element_type=jnp.float32)
    x1 = x_ref[...] + g1_ref[0] * m
    x1_ref[...] = x1
    inv = lax.rsqrt(jnp.mean(x1 * x1, axis=-1, keepdims=True) + NORM_EPS)
    u2_ref[...] = (x1 * inv * gs2_ref[0] + sh2_ref[0]).astype(jnp.bfloat16)


def out_proj(mix_parts, w_bf16, x2d, g1, gs2, sh2, seq_len, tm=ROW_TILE):
    M, D = x2d.shape
    K = w_bf16.shape[0]
    bm = g1.shape[0]
    mod_idx = (lambda i: (i * tm // seq_len, 0, 0)) if bm > 1 else (lambda i: (0, 0, 0))
    mod_spec = pl.BlockSpec((1, 1, D), mod_idx)
    part_spec = pl.BlockSpec((tm, GROUP_W), lambda i: (i, 0))
    return pl.pallas_call(
        _out_proj_kernel,
        grid=(M // tm,),
        in_specs=[
            part_spec, part_spec, part_spec, part_spec, part_spec, part_spec,
            pl.BlockSpec((K, D), lambda i: (0, 0)),
            pl.BlockSpec((tm, D), lambda i: (i, 0)),
            mod_spec, mod_spec, mod_spec,
        ],
        out_specs=[pl.BlockSpec((tm, D), lambda i: (i, 0)), pl.BlockSpec((tm, D), lambda i: (i, 0))],
        out_shape=[jax.ShapeDtypeStruct((M, D), jnp.float32), jax.ShapeDtypeStruct((M, D), jnp.bfloat16)],
        compiler_params=_compiler_params(("parallel",)),
        name="out_proj",
    )(*mix_parts, w_bf16, x2d, g1, gs2, sh2)


def _conv_ffn_kernel(u_ref, up_ref, un_ref, wv_ref, wg_ref, cwv_ref, cwg_ref, cbv_ref, cbg_ref,
                     dn_ref, x1_ref, g2_ref, o_ref, *, tm, seq_len):
    i = pl.program_id(0)
    j = pl.program_id(1)
    u = u_ref[...]
    u_prev = up_ref[...]
    u_next = un_ref[...]
    pos = (lax.broadcasted_iota(jnp.int32, (tm, 1), 0) + i * tm) % seq_len
    row = lax.broadcasted_iota(jnp.int32, (tm, 1), 0)
    has_prev = pos != 0
    has_next = pos != seq_len - 1

    def branch(w_ref, cw_ref, cb_ref):
        w = w_ref[...]
        h = jnp.dot(u, w, preferred_element_type=jnp.float32)
        h_before = jnp.dot(u_prev, w, preferred_element_type=jnp.float32)[HALO_ROWS - 1:HALO_ROWS]
        h_after = jnp.dot(u_next, w, preferred_element_type=jnp.float32)[0:1]
        h_m1 = jnp.where(row == 0, h_before, pltpu.roll(h, 1, axis=0))
        h_p1 = jnp.where(row == tm - 1, h_after, pltpu.roll(h, tm - 1, axis=0))
        h_m1 = jnp.where(has_prev, h_m1, 0.0)
        h_p1 = jnp.where(has_next, h_p1, 0.0)
        cw = cw_ref[...]
        return cw[0:1] * h_m1 + cw[1:2] * h + cw[2:3] * h_p1 + cb_ref[...]

    val = branch(wv_ref, cwv_ref, cbv_ref)
    gate = branch(wg_ref, cwg_ref, cbg_ref)
    a = (gate * jax.nn.sigmoid(gate) * val).astype(jnp.bfloat16)
    part = jnp.dot(a, dn_ref[...], preferred_element_type=jnp.float32)

    @pl.when(j == 0)
    def _():
        o_ref[...] = part

    @pl.when(j > 0)
    def _():
        o_ref[...] += part

    @pl.when(j == pl.num_programs(1) - 1)
    def _():
        o_ref[...] = x1_ref[...] + g2_ref[0] * o_ref[...]


def conv_ffn(u2, up_bf16, conv_w, conv_b, down_bf16, x1, g2, seq_len, tm=ROW_TILE, tf=FFN_TILE_F):
    M, D = x1.shape
    F = down_bf16.shape[0]
    nf = F // tf
    bm = g2.shape[0]
    hb = tm // HALO_ROWS
    n_halo = M // HALO_ROWS
    mod_idx = (lambda i, j: (i * tm // seq_len, 0, 0)) if bm > 1 else (lambda i, j: (0, 0, 0))
    conv_b2 = conv_b.reshape(1, 2 * F)
    kern = functools.partial(_conv_ffn_kernel, tm=tm, seq_len=seq_len)
    return pl.pallas_call(
        kern,
        grid=(M // tm, nf),
        in_specs=[
            pl.BlockSpec((tm, D), lambda i, j: (i, 0)),
            pl.BlockSpec((HALO_ROWS, D), lambda i, j: (jnp.maximum(i * hb - 1, 0), 0)),
            pl.BlockSpec((HALO_ROWS, D), lambda i, j: (jnp.minimum((i + 1) * hb, n_halo - 1), 0)),
            pl.BlockSpec((D, tf), lambda i, j: (0, j)),
            pl.BlockSpec((D, tf), lambda i, j: (0, j + nf)),
            pl.BlockSpec((FFN_CONV, tf), lambda i, j: (0, j)),
            pl.BlockSpec((FFN_CONV, tf), lambda i, j: (0, j + nf)),
            pl.BlockSpec((1, tf), lambda i, j: (0, j)),
            pl.BlockSpec((1, tf), lambda i, j: (0, j + nf)),
            pl.BlockSpec((tf, D), lambda i, j: (j, 0)),
            pl.BlockSpec((tm, D), lambda i, j: (i, 0)),
            pl.BlockSpec((1, 1, D), mod_idx),
        ],
        out_specs=pl.BlockSpec((tm, D), lambda i, j: (i, 0)),
        out_shape=jax.ShapeDtypeStruct((M, D), jnp.float32),
        compiler_params=_compiler_params(("parallel", "arbitrary")),
        name="conv_ffn",
    )(u2, u2, u2, up_bf16, up_bf16, conv_w, conv_w, conv_b2, conv_b2, down_bf16, x1, g2)


def _segment_scan(a, b, reverse):
    n = a.shape[0]
    row = lax.broadcasted_iota(jnp.int32, (n, 1), 0)
    s = 1
    while s < n:
        shift = n - s if reverse else s
        valid = (row < n - s) if reverse else (row >= s)
        a_sh = jnp.where(valid, pltpu.roll(a, shift, axis=0), 1.0)
        b_sh = jnp.where(valid, pltpu.roll(b, shift, axis=0), 0.0)
        b = b + a * b_sh
        a = a * a_sh
        s *= 2
    return a, b


def _rglru_kernel(x_ref, g_ref, cw_ref, cb_ref, w_ref, gb_ref, ca_ref, h0_ref, y_ref, hT_ref, rx_s, hf_s,
                  *, seq_len, chunk):
    n_chunks = seq_len // chunk
    row = lax.broadcasted_iota(jnp.int32, (chunk, 1), 0)
    cw = cw_ref[...]
    cb = cb_ref[...]

    def conv_body(k, carry):
        c0 = pl.multiple_of(k * chunk, chunk)
        x = x_ref[pl.ds(c0, chunk), :]
        before = x_ref[pl.ds(pl.multiple_of(jnp.maximum(c0 - 8, 0), 8), 8), :]
        after = x_ref[pl.ds(pl.multiple_of(jnp.minimum(c0 + chunk, seq_len - 8), 8), 8), :]
        pos = row + c0
        x_m1 = jnp.where(row == 0, before[7:8], pltpu.roll(x, 1, axis=0))
        x_p1 = jnp.where(row == chunk - 1, after[0:1], pltpu.roll(x, chunk - 1, axis=0))
        x_p2 = jnp.where(row == chunk - 2, after[0:1],
                         jnp.where(row == chunk - 1, after[1:2], pltpu.roll(x, chunk - 2, axis=0)))
        x_m1 = jnp.where(pos >= 1, x_m1, 0.0)
        x_p1 = jnp.where(pos < seq_len - 1, x_p1, 0.0)
        x_p2 = jnp.where(pos < seq_len - 2, x_p2, 0.0)
        rx_s[pl.ds(c0, chunk), :] = cw[0:1] * x_m1 + cw[1:2] * x + cw[2:3] * x_p1 + cw[3:4] * x_p2 + cb
        return carry

    lax.fori_loop(0, n_chunks, conv_body, 0)

    def affine(c0, d):
        rx = rx_s[pl.ds(c0, chunk), :]
        w = w_ref[0][:, d * 2 * RG_DH:(d + 1) * 2 * RG_DH]
        gates = jnp.dot(rx.astype(jnp.bfloat16), w, preferred_element_type=jnp.float32)
        gates = gates + gb_ref[0][:, d * 2 * RG_DH:(d + 1) * 2 * RG_DH]
        r = jax.nn.sigmoid(gates[:, :RG_DH])
        i = jax.nn.sigmoid(gates[:, RG_DH:])
        a = jnp.exp(ca_ref[d:d + 1, :] * r)
        b = jnp.sqrt(1.0 - a * a) * (i * rx)
        return a, b

    def fwd_body(k, h):
        c0 = pl.multiple_of(k * chunk, chunk)
        a, b = affine(c0, 0)
        a, b = _segment_scan(a, b, reverse=False)
        hh = b + a * h
        hf_s[pl.ds(c0, chunk), :] = hh
        return hh[chunk - 1:chunk]

    h_f = lax.fori_loop(0, n_chunks, fwd_body, h0_ref[0][0:1, :])

    def bwd_body(kk, h):
        c0 = pl.multiple_of((n_chunks - 1 - kk) * chunk, chunk)
        a, b = affine(c0, 1)
        a, b = _segment_scan(a, b, reverse=True)
        hh = b + a * h
        y_ref[pl.ds(c0, chunk), :] = (hf_s[pl.ds(c0, chunk), :] + hh) * jax.nn.gelu(g_ref[pl.ds(c0, chunk), :])
        return hh[0:1]

    h_b = lax.fori_loop(0, n_chunks, bwd_body, h0_ref[0][1:2, :])
    hT_ref[0] = jnp.concatenate([h_f, h_b], axis=0)


RG_X_COL = 3 * HY_W
RG_GATE_COL = RG_X_COL + RG_W


def rglru(z2d, lp, h0, seq_len, chunk=128):
    M = z2d.shape[0]
    B = M // seq_len
    f32 = jnp.float32
    gw = lp['rg_gate_w']
    w = jnp.transpose(gw, (2, 3, 0, 1, 4)).reshape(RG_H, RG_DH, 4 * RG_DH).astype(jnp.bfloat16)
    gb = jnp.transpose(lp['rg_gate_b'].reshape(2, 2, RG_H, RG_DH), (2, 0, 1, 3)).reshape(RG_H, 1, 4 * RG_DH)
    ca = -RG_C * jax.nn.softplus(-lp['rg_lambda'].astype(f32))
    xcol = RG_X_COL // RG_DH
    gcol = RG_GATE_COL // RG_DH
    kern = functools.partial(_rglru_kernel, seq_len=seq_len, chunk=chunk)
    return pl.pallas_call(
        kern,
        grid=(B, RG_H),
        in_specs=[
            pl.BlockSpec((seq_len, RG_DH), lambda b, h: (b, xcol + h)),
            pl.BlockSpec((seq_len, RG_DH), lambda b, h: (b, gcol + h)),
            pl.BlockSpec((RG_CONV, RG_DH), lambda b, h: (0, h)),
            pl.BlockSpec((1, RG_DH), lambda b, h: (0, h)),
            pl.BlockSpec((1, RG_DH, 4 * RG_DH), lambda b, h: (h, 0, 0)),
            pl.BlockSpec((1, 1, 4 * RG_DH), lambda b, h: (h, 0, 0)),
            pl.BlockSpec((2, RG_DH), lambda b, h: (0, h)),
            pl.BlockSpec((1, 2, RG_DH), lambda b, h: (b, 0, h)),
        ],
        out_specs=[
            pl.BlockSpec((seq_len, RG_DH), lambda b, h: (b, h)),
            pl.BlockSpec((1, 2, RG_DH), lambda b, h: (b, 0, h)),
        ],
        out_shape=[jax.ShapeDtypeStruct((M, RG_W), f32), jax.ShapeDtypeStruct((B, 2, RG_W), f32)],
        scratch_shapes=[pltpu.VMEM((seq_len, RG_DH), f32), pltpu.VMEM((seq_len, RG_DH), f32)],
        compiler_params=_compiler_params(("parallel", "parallel")),
        name="rglru",
    )(z2d, z2d, lp['rg_conv_w'], lp['rg_conv_b'].reshape(1, RG_W), w, gb, ca, h0.astype(f32))


HG_BLOCK = 256


def _chunk_cumsum(x, reverse):
    n = x.shape[0]
    r = lax.broadcasted_iota(jnp.int32, (n, 1), 0) % HG_CHUNK
    s = 1
    while s < HG_CHUNK:
        shift = n - s if reverse else s
        valid = (r < HG_CHUNK - s) if reverse else (r >= s)
        x = x + jnp.where(valid, pltpu.roll(x, shift, axis=0), 0.0)
        s *= 2
    return x


def _hgrn_block(q, v, graw, lb, s_t, mask, reverse):
    bf16 = jnp.bfloat16
    rows = q.shape[0]
    nseg = rows // HG_CHUNK
    f = lb + (1.0 - lb) * jax.nn.sigmoid(graw)
    k = 1.0 - f
    bc = _chunk_cumsum(jnp.log(f), reverse)
    mid = HG_CHUNK // 2 - 1
    ref_row = HG_CHUNK - 1 - mid if reverse else mid
    tot_row = 0 if reverse else HG_CHUNK - 1
    qs, ks, kd, tots = [], [], [], []
    for j in range(nseg):
        sl = slice(j * HG_CHUNK, (j + 1) * HG_CHUNK)
        g = bc[sl]
        ref = g[ref_row:ref_row + 1]
        tot = g[tot_row:tot_row + 1]
        qs.append(q[sl] * jnp.exp(g - ref))
        ks.append(k[sl] * jnp.exp(ref - g))
        kd.append(k[sl] * jnp.exp(tot - g))
        tots.append(jnp.exp(tot))
    qs = jnp.concatenate(qs, axis=0).astype(bf16)
    ks = jnp.concatenate(ks, axis=0).astype(bf16)
    att = lax.dot_general(qs, ks, (((1,), (1,)), ((), ())), preferred_element_type=jnp.float32)
    att = jnp.where(mask, att, 0.0).astype(bf16)
    v16 = v.astype(bf16)
    o_intra = jnp.dot(att, v16, preferred_element_type=jnp.float32)
    qd = (q * jnp.exp(bc)).astype(bf16)
    outs = [None] * nseg
    for j in (range(nseg - 1, -1, -1) if reverse else range(nseg)):
        sl = slice(j * HG_CHUNK, (j + 1) * HG_CHUNK)
        inter = lax.dot_general(qd[sl], s_t.astype(bf16), (((1,), (1,)), ((), ())),
                                preferred_element_type=jnp.float32)
        outs[j] = o_intra[sl] + inter
        kv_t = lax.dot_general(v16[sl], kd[j].astype(bf16), (((0,), (0,)), ((), ())),
                               preferred_element_type=jnp.float32)
        s_t = tots[j] * s_t + kv_t
    return jnp.concatenate(outs, axis=0), s_t


def _hgrn_kernel(q_ref, v_ref, ff_ref, fb_ref, gate_ref, lb_ref, ng_ref, s0_ref, y_ref, sT_ref, of_s,
                 *, seq_len, block):
    n_blocks = seq_len // block
    r = lax.broadcasted_iota(jnp.int32, (block, block), 0)
    c = lax.broadcasted_iota(jnp.int32, (block, block), 1)
    same = (r // HG_CHUNK) == (c // HG_CHUNK)
    mask_f = same & (c <= r)
    mask_b = same & (c >= r)

    def fwd_body(i, s_t):
        c0 = pl.multiple_of(i * block, block)
        o, s_t = _hgrn_block(q_ref[pl.ds(c0, block), :], v_ref[pl.ds(c0, block), :], ff_ref[pl.ds(c0, block), :],
                             lb_ref[0:1, :], s_t, mask_f, reverse=False)
        of_s[pl.ds(c0, block), :] = o
        return s_t

    s_f = lax.fori_loop(0, n_blocks, fwd_body, s0_ref[0, 0, 0].T)

    def bwd_body(ii, s_t):
        c0 = pl.multiple_of((n_blocks - 1 - ii) * block, block)
        o, s_t = _hgrn_block(q_ref[pl.ds(c0, block), :], v_ref[pl.ds(c0, block), :], fb_ref[pl.ds(c0, block), :],
                             lb_ref[1:2, :], s_t, mask_b, reverse=True)
        o = o + of_s[pl.ds(c0, block), :]
        inv = lax.rsqrt(jnp.mean(o * o, axis=-1, keepdims=True) + NORM_EPS)
        gate = gate_ref[pl.ds(c0, block), :]
        y_ref[pl.ds(c0, block), :] = o * inv * ng_ref[...] * (gate * jax.nn.sigmoid(gate))
        return s_t

    s_b = lax.fori_loop(0, n_blocks, bwd_body, s0_ref[0, 1, 0].T)
    sT_ref[0, 0, 0] = s_f.T
    sT_ref[0, 1, 0] = s_b.T


HG_Q_COL = RG_GATE_COL + RG_W


def hgrn2(z2d, lp, lb, s0, seq_len):
    M = z2d.shape[0]
    B = M // seq_len
    f32 = jnp.float32
    block = min(HG_BLOCK, seq_len)
    col = HG_Q_COL // HG_DK
    kern = functools.partial(_hgrn_kernel, seq_len=seq_len, block=block)

    def zspec(part):
        return pl.BlockSpec((seq_len, HG_DK), lambda b, h: (b, col + part * HG_H + h))

    return pl.pallas_call(
        kern,
        grid=(B, HG_H),
        in_specs=[
            zspec(0), zspec(1), zspec(2), zspec(3), zspec(4),
            pl.BlockSpec((2, HG_DK), lambda b, h: (0, h)),
            pl.BlockSpec((1, HG_DV), lambda b, h: (0, h)),
            pl.BlockSpec((1, 2, 1, HG_DK, HG_DV), lambda b, h: (b, 0, h, 0, 0)),
        ],
        out_specs=[
            pl.BlockSpec((seq_len, HG_DV), lambda b, h: (b, h)),
            pl.BlockSpec((1, 2, 1, HG_DK, HG_DV), lambda b, h: (b, 0, h, 0, 0)),
        ],
        out_shape=[jax.ShapeDtypeStruct((M, HG_W), f32), jax.ShapeDtypeStruct((B, 2, HG_H, HG_DK, HG_DV), f32)],
        scratch_shapes=[pltpu.VMEM((seq_len, HG_DV), f32)],
        compiler_params=_compiler_params(("parallel", "parallel")),
        name="hgrn2",
    )(z2d, z2d, z2d, z2d, z2d, lb.astype(f32), lp['hg_norm_g'].reshape(1, HG_W), s0.astype(f32))


ATT_Q_COL = HG_Q_COL + 5 * HG_W
ATT_KV_COL = ATT_Q_COL + ATT_W
LANES = 128
ATT_TQ = 512


def _head_mean_square(x):
    ss = x * x
    hi = ss.astype(jnp.bfloat16)
    lo = (ss - hi.astype(jnp.float32)).astype(jnp.bfloat16)
    r = lax.broadcasted_iota(jnp.int32, (LANES, LANES), 0) // HEAD_DIM
    c = lax.broadcasted_iota(jnp.int32, (LANES, LANES), 1) // HEAD_DIM
    ones = jnp.where(r == c, 1.0, 0.0).astype(jnp.bfloat16)
    tot = jnp.dot(hi, ones, preferred_element_type=jnp.float32) + jnp.dot(lo, ones, preferred_element_type=jnp.float32)
    return tot * (1.0 / HEAD_DIM)


def _qk_norm_rope(x, gain, cos, sin):
    y = x * lax.rsqrt(_head_mean_square(x) + NORM_EPS) * gain
    if cos is None:
        return y
    lane = lax.broadcasted_iota(jnp.int32, (1, LANES), 1) % (HEAD_DIM // 2)
    quarter = HEAD_DIM // 4
    partner = jnp.where(lane < quarter, pltpu.roll(y, LANES - quarter, axis=1), pltpu.roll(y, quarter, axis=1))
    return y * cos + partner * sin


def _both_halves(x, g):
    lane = lax.broadcasted_iota(jnp.int32, (1, LANES), 1)
    keep = (lane < HEAD_DIM) if g == 0 else (lane >= HEAD_DIM)
    return jnp.where(keep, x, pltpu.roll(x, HEAD_DIM, axis=1))


def _attn_kernel(*refs, latent, tq, seq_len):
    bf16 = jnp.bfloat16
    if latent:
        (q_ref, kvm_ref, kvp_ref, kvn_ref, cosm_ref, sinm_ref, cosp_ref, sinp_ref, cosn_ref, sinn_ref,
         ck_ref, cv_ref, qg_ref, kg_ref, sink_ref, y_ref, q_s, k2_s, v2_s, ck2_s, cv2_s) = refs
    else:
        (q_ref, kvm_ref, qg_ref, kg_ref, sink_ref, y_ref, ka_ref, q_s, k2_s, v2_s) = refs
    i = pl.program_id(1)
    blk = ATT_BLOCK

    if latent:
        pieces = [(kvp_ref, cosp_ref, sinp_ref, 0, blk), (kvm_ref, cosm_ref, sinm_ref, blk, tq),
                  (kvn_ref, cosn_ref, sinn_ref, blk + tq, blk)]
    else:
        pieces = [(kvm_ref, None, None, 0, tq)]
    for kv_ref, cos_ref, sin_ref, r0, n in pieces:
        kv = kv_ref[...]
        kn = _qk_norm_rope(kv[:, :LANES], kg_ref[...], None if cos_ref is None else cos_ref[...],
                           None if sin_ref is None else sin_ref[...])
        if not latent:
            ka_ref[...] = kn
        for g in range(ATT_KV):
            k2_s[g, r0:r0 + n, :] = _both_halves(kn, g).astype(bf16)
            v2_s[g, r0:r0 + n, :] = _both_halves(kv[:, LANES:], g).astype(bf16)
    if latent:
        for g in range(ATT_KV):
            ck2_s[g] = _both_halves(ck_ref[0], g).astype(bf16)
            cv2_s[g] = _both_halves(cv_ref[0], g).astype(bf16)

    for p in range(ATT_W // LANES):
        qn = _qk_norm_rope(q_ref[:, p * LANES:(p + 1) * LANES], qg_ref[...],
                           cosm_ref[...] if latent else None, sinm_ref[...] if latent else None)
        q_s[:, p * LANES:(p + 1) * LANES] = (qn * ATT_SCALE).astype(bf16)

    rows = ATT_G * blk
    lane = lax.broadcasted_iota(jnp.int32, (1, LANES), 1)
    low = lane < HEAD_DIM
    head_of_row = lax.broadcasted_iota(jnp.int32, (rows, 1), 0) // blk
    nt = (((1,), (1,)), ((), ()))
    for j in range(tq // blk):
        for g in range(ATT_KV):
            stack = []
            for p in (2 * g, 2 * g + 1):
                qb = q_s[j * blk:(j + 1) * blk, p * LANES:(p + 1) * LANES]
                stack.append(jnp.where(low, qb, jnp.zeros_like(qb)))
                stack.append(jnp.where(low, jnp.zeros_like(qb), qb))
            qg = jnp.concatenate(stack, axis=0)
            sink = jnp.zeros((rows, 1), jnp.float32)
            for hh in range(ATT_G):
                sink = jnp.where(head_of_row == hh, sink_ref[ATT_G * g + hh:ATT_G * g + hh + 1, 0:1], sink)
            if latent:
                nk = 3 * blk
                s_w = lax.dot_general(qg, k2_s[g, j * blk:j * blk + nk, :], nt, preferred_element_type=jnp.float32)
                a = lax.broadcasted_iota(jnp.int32, (rows, nk), 0) % blk
                c = lax.broadcasted_iota(jnp.int32, (rows, nk), 1)
                kpos = c + (i * tq + j * blk - blk)
                ok = (jnp.abs(c - blk - a) <= WINDOW) & (kpos >= 0) & (kpos < seq_len)
                s_w = jnp.where(ok, s_w, NEG_INF)
                s_c = lax.dot_general(qg, ck2_s[g], nt, preferred_element_type=jnp.float32)
                m = jnp.maximum(jnp.maximum(jnp.max(s_w, axis=-1, keepdims=True),
                                            jnp.max(s_c, axis=-1, keepdims=True)), sink)
                e_w = jnp.exp(s_w - m)
                e_c = jnp.exp(s_c - m)
                den = jnp.sum(e_w, axis=-1, keepdims=True) + jnp.sum(e_c, axis=-1, keepdims=True) + jnp.exp(sink - m)
                o = (jnp.dot(e_w.astype(bf16), v2_s[g, j * blk:j * blk + nk, :], preferred_element_type=jnp.float32)
                     + jnp.dot(e_c.astype(bf16), cv2_s[g], preferred_element_type=jnp.float32))
            else:
                s = lax.dot_general(qg, k2_s[g], nt, preferred_element_type=jnp.float32)
                m = jnp.maximum(jnp.max(s, axis=-1, keepdims=True), sink)
                e = jnp.exp(s - m)
                den = jnp.sum(e, axis=-1, keepdims=True) + jnp.exp(sink - m)
                o = jnp.dot(e.astype(bf16), v2_s[g], preferred_element_type=jnp.float32)
            o = o / den
            for t, p in enumerate((2 * g, 2 * g + 1)):
                y_ref[j * blk:(j + 1) * blk, p * LANES:(p + 1) * LANES] = jnp.where(
                    low, o[2 * t * blk:(2 * t + 1) * blk], o[(2 * t + 1) * blk:(2 * t + 2) * blk])


def _rope_tables(seq_len):
    f32 = jnp.float32
    half = HEAD_DIM // 2
    t = jnp.arange(seq_len)
    row = (t // GRID_W).astype(f32)
    col = (t % GRID_W).astype(f32)
    inv = ROPE_BASE ** (-jnp.arange(0, half, 2, dtype=f32) / half)

    def tab(pos):
        ang = pos[:, None] * inv[None, :]
        return (jnp.concatenate([jnp.cos(ang), jnp.cos(ang)], axis=-1),
                jnp.concatenate([-jnp.sin(ang), jnp.sin(ang)], axis=-1))

    cr, sr = tab(row)
    cc, sc = tab(col)
    cos = jnp.concatenate([cr, cc], axis=-1)
    sin = jnp.concatenate([sr, sc], axis=-1)
    return jnp.tile(cos, (1, 2)), jnp.tile(sin, (1, 2))


def attention(z2d, lp, seq_len, ctx_k=None, ctx_v=None):
    M = z2d.shape[0]
    B = M // seq_len
    f32 = jnp.float32
    bf16 = jnp.bfloat16
    latent = ctx_k is not None
    tq = min(ATT_TQ, seq_len)
    nq = seq_len // tq
    qg = jnp.tile(lp['att_qn_g'].astype(f32), 2).reshape(1, LANES)
    kg = jnp.tile(lp['att_kn_g'].astype(f32), 2).reshape(1, LANES)
    sink = jnp.broadcast_to(lp['att_sink'].astype(f32)[:, None], (ATT_H, LANES))
    qcol = ATT_Q_COL // ATT_W
    kvcol = ATT_KV_COL // (2 * LANES)
    kern = functools.partial(_attn_kernel, latent=latent, tq=tq, seq_len=seq_len)
    q_spec = pl.BlockSpec((tq, ATT_W), lambda b, i: (b * nq + i, qcol))
    kv_spec = pl.BlockSpec((tq, 2 * LANES), lambda b, i: (b * nq + i, kvcol))
    small = lambda shape: pl.BlockSpec(shape, lambda b, i: (0, 0))
    y_spec = pl.BlockSpec((tq, ATT_W), lambda b, i: (b * nq + i, 0))
    params = _compiler_params(("parallel", "parallel"))
    if not latent:
        return pl.pallas_call(
            kern, grid=(B, nq),
            in_specs=[q_spec, kv_spec, small((1, LANES)), small((1, LANES)), small((ATT_H, LANES))],
            out_specs=[y_spec, pl.BlockSpec((tq, LANES), lambda b, i: (b * nq + i, 0))],
            out_shape=[jax.ShapeDtypeStruct((M, ATT_W), f32), jax.ShapeDtypeStruct((M, LANES), f32)],
            scratch_shapes=[pltpu.VMEM((tq, ATT_W), bf16), pltpu.VMEM((ATT_KV, tq, LANES), bf16),
                            pltpu.VMEM((ATT_KV, tq, LANES), bf16)],
            compiler_params=params, name="attn_context",
        )(z2d, z2d, qg, kg, sink)
    P = ctx_k.shape[1]
    hb = tq // ATT_BLOCK
    nblk = seq_len // ATT_BLOCK
    cos, sin = _rope_tables(seq_len)
    prev_idx = lambda b, i: (b * nblk + jnp.maximum(i * hb - 1, 0), kvcol)
    next_idx = lambda b, i: (b * nblk + jnp.minimum((i + 1) * hb, nblk - 1), kvcol)
    tab_m = pl.BlockSpec((tq, LANES), lambda b, i: (i, 0))
    tab_p = pl.BlockSpec((ATT_BLOCK, LANES), lambda b, i: (jnp.maximum(i * hb - 1, 0), 0))
    tab_n = pl.BlockSpec((ATT_BLOCK, LANES), lambda b, i: (jnp.minimum((i + 1) * hb, nblk - 1), 0))
    ctx_spec = pl.BlockSpec((1, P, LANES), lambda b, i: (b, 0, 0))
    nk_all = tq + 2 * ATT_BLOCK
    return pl.pallas_call(
        kern, grid=(B, nq),
        in_specs=[q_spec, kv_spec, pl.BlockSpec((ATT_BLOCK, 2 * LANES), prev_idx),
                  pl.BlockSpec((ATT_BLOCK, 2 * LANES), next_idx),
                  tab_m, tab_m, tab_p, tab_p, tab_n, tab_n, ctx_spec, ctx_spec,
                  small((1, LANES)), small((1, LANES)), small((ATT_H, LANES))],
        out_specs=y_spec,
        out_shape=jax.ShapeDtypeStruct((M, ATT_W), f32),
        scratch_shapes=[pltpu.VMEM((tq, ATT_W), bf16), pltpu.VMEM((ATT_KV, nk_all, LANES), bf16),
                        pltpu.VMEM((ATT_KV, nk_all, LANES), bf16), pltpu.VMEM((ATT_KV, P, LANES), bf16),
                        pltpu.VMEM((ATT_KV, P, LANES), bf16)],
        compiler_params=params, name="attn_latent",
    )(z2d, z2d, z2d, z2d, cos, sin, cos, sin, cos, sin, ctx_k.astype(f32), ctx_v.astype(f32), qg, kg, sink)


HY_CHUNK = 256
HY_TILE_F = 256


def _hy_pre_kernel(hv_ref, h1_ref, h2_ref, cw_ref, cb_ref, skip_ref, w_ref, x1_ref, base_ref, *, seq_len, chunk):
    n_chunks = seq_len // chunk
    row = lax.broadcasted_iota(jnp.int32, (chunk, 1), 0)

    def conv(ref, part, c0):
        x = ref[pl.ds(c0, chunk), :]
        before = ref[pl.ds(pl.multiple_of(jnp.maximum(c0 - 8, 0), 8), 8), :]
        after = ref[pl.ds(pl.multiple_of(jnp.minimum(c0 + chunk, seq_len - 8), 8), 8), :]
        pos = row + c0
        x_m1 = jnp.where(row == 0, before[7:8], pltpu.roll(x, 1, axis=0))
        x_p1 = jnp.where(row == chunk - 1, after[0:1], pltpu.roll(x, chunk - 1, axis=0))
        x_m1 = jnp.where(pos >= 1, x_m1, 0.0)
        x_p1 = jnp.where(pos < seq_len - 1, x_p1, 0.0)
        cw = cw_ref[part]
        return cw[0:1] * x_m1 + cw[1:2] * x + cw[2:3] * x_p1 + cb_ref[part]

    def body(k, carry):
        c0 = pl.multiple_of(k * chunk, chunk)
        hv = conv(hv_ref, 0, c0)
        hx1 = conv(h1_ref, 1, c0)
        hx2 = conv(h2_ref, 2, c0)
        w = hv * hx2
        w_ref[pl.ds(c0, chunk), :] = w.astype(jnp.bfloat16)
        x1_ref[pl.ds(c0, chunk), :] = hx1
        base_ref[pl.ds(c0, chunk), :] = hx1 * (w * skip_ref[...])
        return carry

    lax.fori_loop(0, n_chunks, body, 0)


def hyena_pre(z2d, lp, seq_len):
    M = z2d.shape[0]
    B = M // seq_len
    f32 = jnp.float32
    nblk = HY_W // LANES
    chunk = min(HY_CHUNK, seq_len)
    cw = lp['hy_conv_w'].reshape(HY_SHORT, 3, HY_W).transpose(1, 0, 2)
    cb = lp['hy_conv_b'].reshape(3, 1, HY_W)
    kern = functools.partial(_hy_pre_kernel, seq_len=seq_len, chunk=chunk)
    zspec = lambda part: pl.BlockSpec((seq_len, LANES), lambda b, c: (b, part * nblk + c))
    ospec = pl.BlockSpec((seq_len, LANES), lambda b, c: (b, c))
    return pl.pallas_call(
        kern, grid=(B, nblk),
        in_specs=[zspec(0), zspec(1), zspec(2),
                  pl.BlockSpec((3, HY_SHORT, LANES), lambda b, c: (0, 0, c)),
                  pl.BlockSpec((3, 1, LANES), lambda b, c: (0, 0, c)),
                  pl.BlockSpec((1, LANES), lambda b, c: (0, c))],
        out_specs=[ospec, ospec, ospec],
        out_shape=[jax.ShapeDtypeStruct((M, HY_W), jnp.bfloat16), jax.ShapeDtypeStruct((M, HY_W), f32),
                   jax.ShapeDtypeStruct((M, HY_W), f32)],
        compiler_params=_compiler_params(("parallel", "parallel")),
        name="hyena_pre",
    )(z2d, z2d, z2d, cw, cb, lp['hy_skip'].reshape(1, HY_W).astype(f32))


def _hy_conv_kernel(g_ref, gi_ref, w_ref, kra_ref, krb_ref, ki_ref, o_ref, *, tf, seq_len):
    j = pl.program_id(1)
    x = jnp.dot(g_ref[0], w_ref[...], preferred_element_type=jnp.float32)
    xr, xs = x[:tf], x[tf:]
    ki = ki_ref[...]
    a = xr * kra_ref[...] + xs * ki
    b = xs * krb_ref[...] - xr * ki
    ab = (jnp.concatenate([a, b], axis=0) * (1.0 / seq_len)).astype(jnp.bfloat16)
    part = jnp.dot(gi_ref[0], ab, preferred_element_type=jnp.float32)

    @pl.when(j == 0)
    def _():
        o_ref[...] = part

    @pl.when(j > 0)
    def _():
        o_ref[...] += part


def _dft_tables(seq_len, tf):
    n = 2 * seq_len
    nf = seq_len // tf
    f = jnp.arange(seq_len, dtype=jnp.int32)
    prod = (f[:, None] * f[None, :]) % n
    ang = prod.astype(jnp.float32) * (2.0 * math.pi / n)
    c = jnp.cos(ang)
    alt = (1 - 2 * (f % 2)).astype(jnp.float32)
    s = jnp.where(f[:, None] == 0, alt[None, :], jnp.sin(ang))
    fwd = jnp.concatenate([c.reshape(nf, tf, seq_len), s.reshape(nf, tf, seq_len)], axis=1)
    inv = jnp.concatenate([c.reshape(seq_len, nf, tf), s.T.reshape(seq_len, nf, tf)], axis=2).transpose(1, 0, 2)
    return fwd.astype(jnp.bfloat16), inv.astype(jnp.bfloat16)


def hyena_long_conv(w_bf16, k_long, seq_len):
    M, W = w_bf16.shape
    B = M // seq_len
    f32 = jnp.float32
    tf = min(HY_TILE_F, seq_len)
    nf = seq_len // tf
    fwd, inv = _dft_tables(seq_len, tf)
    kf = jnp.fft.rfft(k_long, axis=0)
    kr = jnp.real(kf[:seq_len]).astype(f32)
    ki = jnp.imag(kf[:seq_len]).astype(f32).at[0].set(0.0)
    kra = kr.at[0].multiply(0.5)
    krb = kr.at[0].set(0.5 * jnp.real(kf[seq_len]).astype(f32))
    kern = functools.partial(_hy_conv_kernel, tf=tf, seq_len=seq_len)
    kspec = pl.BlockSpec((tf, W), lambda b, j: (j, 0))
    return pl.pallas_call(
        kern, grid=(B, nf),
        in_specs=[pl.BlockSpec((1, 2 * tf, seq_len), lambda b, j: (j, 0, 0)),
                  pl.BlockSpec((1, seq_len, 2 * tf), lambda b, j: (j, 0, 0)),
                  pl.BlockSpec((seq_len, W), lambda b, j: (b, 0)),
                  kspec, kspec, kspec],
        out_specs=pl.BlockSpec((seq_len, W), lambda b, j: (b, 0)),
        out_shape=jax.ShapeDtypeStruct((M, W), f32),
        compiler_params=_compiler_params(("parallel", "arbitrary")),
        name="hyena_long_conv",
    )(fwd, inv, w_bf16, kra, krb, ki)


def rms_norm(x, g, eps=NORM_EPS):
    xf = x.astype(jnp.float32)
    y = xf * lax.rsqrt(jnp.mean(xf * xf, axis=-1, keepdims=True) + eps)
    return (y * g.astype(jnp.float32)).astype(x.dtype)


def dwconv(x, w, b, pad):
    y = lax.conv_general_dilated(x, w.astype(x.dtype)[:, None, :], window_strides=(1,), padding=[pad],
                                 dimension_numbers=('NWC', 'WIO', 'NWC'), feature_group_count=x.shape[-1])
    return y + b.astype(x.dtype)


def hyena_kernel(L, w1, b1, freq, w2, b2, w3, b3, decay):
    f32 = jnp.float32
    t_idx = jnp.arange(L, dtype=f32)
    t = t_idx / (L - 1)
    ang = (2.0 * math.pi * t_idx / L)[:, None] * jnp.linspace(1e-4, HY_BANDS - 1, HY_BANDS, dtype=f32)[None, :]
    z = jnp.concatenate([t[:, None], jnp.cos(ang), -jnp.sin(ang)], axis=-1)
    h = jnp.sin(freq[0].astype(f32) * (z @ w1.astype(f32) + b1.astype(f32)))
    h = jnp.sin(freq[1].astype(f32) * (h @ w2.astype(f32) + b2.astype(f32)))
    h = (h @ w3.astype(f32) + b3.astype(f32)) * jnp.exp(-t[:, None] * jnp.abs(decay.astype(f32)))
    h_fwd, h_bwd = h[:, :HY_W], h[:, HY_W:]
    k = jnp.concatenate([h_fwd, jnp.zeros((1, HY_W), f32), h_bwd[:0:-1]], axis=0)
    return k / jnp.sum(jnp.abs(k), axis=0, keepdims=True)


def centred_long_conv(u, k, skip):
    L = u.shape[1]
    spec = jnp.fft.rfft(u, n=2 * L, axis=1) * jnp.fft.rfft(k, axis=0)[None]
    return jnp.fft.irfft(spec, n=2 * L, axis=1)[:, :L] + u * skip


def _affine_combine(e1, e2):
    a1, b1 = e1
    a2, b2 = e2
    return a1 * a2, a2 * b1 + b2


def rglru_scan(x, gate_w, gate_b, lam, h0):
    f32 = jnp.float32
    B, T, W = x.shape
    xh = x.reshape(B, T, RG_H, RG_DH)
    gates = jnp.einsum('bthi,ghij->gbthj', xh, gate_w.astype(f32)).reshape(2, B, T, W) + gate_b.astype(f32)[:, None, None, :]
    r = jax.nn.sigmoid(gates[0])
    i = jax.nn.sigmoid(gates[1])
    log_a = -RG_C * r * jax.nn.softplus(-lam.astype(f32))
    a = jnp.exp(log_a)
    b = jnp.sqrt(-jnp.expm1(2.0 * log_a)) * (i * x)
    b = b.at[:, 0].add(a[:, 0] * h0)
    _, h = lax.associative_scan(_affine_combine, (a, b), axis=1)
    return h, h[:, -1]


def hgrn2_scan(q, k, v, log_f, S0):
    B, T, H, Dk = q.shape
    Dv = v.shape[-1]
    N = T // HG_CHUNK

    def chunks(a):
        return a.reshape(B, N, HG_CHUNK, H, a.shape[-1]).transpose(1, 0, 3, 2, 4)

    causal = jnp.tril(jnp.ones((HG_CHUNK, HG_CHUNK), jnp.float32))
    mid = HG_CHUNK // 2 - 1

    def step(S, inp):
        qc, kc, vc, gc = inp
        bc = jnp.cumsum(gc, axis=2)
        ref = bc[:, :, mid:mid + 1]
        att = jnp.einsum('bhtd,bhsd->bhts', qc * jnp.exp(bc - ref), kc * jnp.exp(ref - bc)) * causal
        o = jnp.einsum('bhts,bhse->bhte', att, vc) + jnp.einsum('bhtd,bhde->bhte', qc * jnp.exp(bc), S)
        bl = bc[:, :, -1:]
        S = jnp.exp(bl)[:, :, 0, :, None] * S + jnp.einsum('bhsd,bhse->bhde', kc * jnp.exp(bl - bc), vc)
        return S, o

    S, o = lax.scan(step, S0, (chunks(q), chunks(k), chunks(v), chunks(log_f)))
    return o.transpose(1, 0, 3, 2, 4).reshape(B, T, H, Dv), S


def axial_rope(x):
    f32 = jnp.float32
    T = x.shape[1]
    rows = T // GRID_W
    row = jnp.repeat(jnp.arange(rows, dtype=f32), GRID_W)
    col = jnp.tile(jnp.arange(GRID_W, dtype=f32), rows)
    half = HEAD_DIM // 2
    inv = ROPE_BASE ** (-jnp.arange(0, half, 2, dtype=f32) / half)

    def rot(xa, pos):
        ang = pos[:, None] * inv[None, :]
        cos = jnp.cos(ang)[None, :, None, :]
        sin = jnp.sin(ang)[None, :, None, :]
        x1, x2 = jnp.split(xa.astype(f32), 2, axis=-1)
        return jnp.concatenate([x1 * cos - x2 * sin, x1 * sin + x2 * cos], axis=-1)

    return jnp.concatenate([rot(x[..., :half], row), rot(x[..., half:], col)], axis=-1).astype(x.dtype)


def sink_attend(qb, keys, vals, bias, sink):
    s = jnp.einsum('bqkgd,bskd->bkgqs', qb, keys).astype(jnp.float32) * ATT_SCALE + bias
    sk = sink[None, :, :, None, None]
    m = jnp.maximum(jnp.max(s, axis=-1, keepdims=True), sk)
    e = jnp.exp(s - m)
    p = e / (jnp.sum(e, axis=-1, keepdims=True) + jnp.exp(sk - m))
    return jnp.einsum('bkgqs,bskd->bqkgd', p.astype(vals.dtype), vals)


def attn_context(q, k, v, sink):
    B, S = q.shape[:2]
    nb = S // ATT_BLOCK
    qb = q.reshape(B, nb, ATT_BLOCK, ATT_KV, ATT_G, HEAD_DIM).swapaxes(0, 1)
    o = lax.map(lambda qq: sink_attend(qq, k, v, 0.0, sink), qb)
    return o.swapaxes(0, 1).reshape(B, S, ATT_W)


def attn_latent(q, k, v, kc, vc, sink):
    B, T = q.shape[:2]
    nb = T // ATT_BLOCK
    P = kc.shape[1]

    def blocks3(a):
        ap = jnp.pad(a, ((0, 0), (ATT_BLOCK, ATT_BLOCK), (0, 0), (0, 0))).reshape(B, nb + 2, ATT_BLOCK, ATT_KV, HEAD_DIM)
        return jnp.concatenate([ap[:, :-2], ap[:, 1:-1], ap[:, 2:]], axis=2).swapaxes(0, 1)

    kl, vl = blocks3(k), blocks3(v)
    qb = q.reshape(B, nb, ATT_BLOCK, ATT_KV, ATT_G, HEAD_DIM).swapaxes(0, 1)
    base = jnp.arange(nb)[:, None, None] * ATT_BLOCK
    qpos = base + jnp.arange(ATT_BLOCK)[None, :, None]
    kpos = base - ATT_BLOCK + jnp.arange(3 * ATT_BLOCK)[None, None, :]
    ok = (jnp.abs(qpos - kpos) <= WINDOW) & (kpos >= 0) & (kpos < T)
    bias = jnp.concatenate([jnp.where(ok, 0.0, NEG_INF).astype(jnp.float32),
                            jnp.zeros((nb, ATT_BLOCK, P), jnp.float32)], axis=-1)

    def one(args):
        qq, kk, vv, bb = args
        return sink_attend(qq, jnp.concatenate([kk, kc], axis=1), jnp.concatenate([vv, vc], axis=1), bb, sink)

    o = lax.map(one, (qb, kl, vl, bias))
    return o.swapaxes(0, 1).reshape(B, T, ATT_W)


def token_mixers(z, lp, lb, ctx):
    f32 = jnp.float32
    B, T, _ = z.shape
    z2d = z.reshape(B * T, -1)

    k_long = hyena_kernel(T, lp['hy_f_w1'], lp['hy_f_b1'], lp['hy_f_freq'], lp['hy_f_w2'], lp['hy_f_b2'],
                          lp['hy_f_w3'], lp['hy_f_b3'], lp['hy_decay'])
    hy_w, hy_x1, hy_base = hyena_pre(z2d, lp, T)
    hy_conv = hyena_long_conv(hy_w, k_long, T)

    h0 = jnp.zeros((B, 2, RG_W), f32) if ctx is None else ctx[2]
    y_rg, hT_rg = rglru(z2d, lp, h0, T)

    S0 = jnp.zeros((B, 2, HG_H, HG_DK, HG_DV), f32) if ctx is None else ctx[3]
    y_hg, S_hg = hgrn2(z2d, lp, lb, S0, T)

    if ctx is None:
        y_att, ka = attention(z2d, lp, T)
        ka = ka.reshape(B, T, ATT_KV, HEAD_DIM)
        va = z[..., ATT_KV_COL + LANES:ATT_KV_COL + 2 * LANES].reshape(B, T, ATT_KV, HEAD_DIM)
        new_ctx = (ka, va, hT_rg, S_hg)
    else:
        P = ctx[0].shape[1]
        y_att = attention(z2d, lp, T, ctx[0].reshape(B, P, ATT_KV * HEAD_DIM), ctx[1].reshape(B, P, ATT_KV * HEAD_DIM))
        new_ctx = None
    return (hy_conv, hy_x1, hy_base, y_rg, y_hg, y_att), new_ctx


def trunk_layer(x, mod, lp, lb, ctx):
    B, T, D = x.shape
    sh1, sc1, g1, sh2, sc2, g2 = [m[:, None, :] for m in jnp.split(mod, 6, axis=-1)]
    gs1 = lp['norm1_g'][None, None, :] * (1.0 + sc1)
    gs2 = lp['norm2_g'][None, None, :] * (1.0 + sc2)
    x2d = x.reshape(B * T, D)
    z = in_proj(x2d, gs1, sh1, lp['w_in_bf16'], T)
    mix_parts, new_ctx = token_mixers(z.reshape(B, T, IN_COLS_PAD), lp, lb, ctx)
    x1, u2 = out_proj(mix_parts, lp['w_out_bf16'], x2d, g1, gs2, sh2, T)
    x2 = conv_ffn(u2, lp['ffn_up_bf16'], lp['ffn_conv_w'], lp['ffn_conv_b'], lp['ffn_down_bf16'], x1, g2, T)
    return x2.reshape(B, T, D), new_ctx


def kernel(x_prompt, x_sample, cache_k, cache_v, state_rglru, state_hgrn, c, c_ctx, norm1_g, norm2_g, ada_w, ada_b, w_in, w_out, hy_conv_w, hy_conv_b, hy_f_w1, hy_f_b1, hy_f_freq, hy_f_w2, hy_f_b2, hy_f_w3, hy_f_b3, hy_decay, hy_skip, rg_conv_w, rg_conv_b, rg_gate_w, rg_gate_b, rg_lambda, hg_lb, hg_norm_g, att_qn_g, att_kn_g, att_sink, ffn_up, ffn_conv_w, ffn_conv_b, ffn_down):
    bf16 = jnp.bfloat16
    stacked = dict(norm1_g=norm1_g, norm2_g=norm2_g, ada_w=ada_w, ada_b=ada_b,
                   hy_conv_w=hy_conv_w, hy_conv_b=hy_conv_b, hy_f_w1=hy_f_w1, hy_f_b1=hy_f_b1, hy_f_freq=hy_f_freq,
                   hy_f_w2=hy_f_w2, hy_f_b2=hy_f_b2, hy_f_w3=hy_f_w3, hy_f_b3=hy_f_b3, hy_decay=hy_decay, hy_skip=hy_skip,
                   rg_conv_w=rg_conv_w, rg_conv_b=rg_conv_b, rg_gate_w=rg_gate_w, rg_gate_b=rg_gate_b, rg_lambda=rg_lambda,
                   hg_norm_g=hg_norm_g, att_qn_g=att_qn_g, att_kn_g=att_kn_g, att_sink=att_sink,
                   ffn_conv_w=ffn_conv_w, ffn_conv_b=ffn_conv_b,
                   w_in_bf16=jnp.pad(w_in.astype(bf16), ((0, 0), (0, 0), (0, IN_COLS_PAD - IN_COLS))),
                   w_out_bf16=w_out.astype(bf16), ffn_up_bf16=ffn_up.astype(bf16), ffn_down_bf16=ffn_down.astype(bf16))
    lbs = jnp.cumsum(jax.nn.softmax(hg_lb.astype(jnp.float32), axis=0), axis=0)
    lbs = lbs - lbs[0:1]
    y_prompt, y_sample = x_prompt, x_sample
    ks, vs, rgs, hgs = [], [], [], []
    for l in range(DEPTH):
        lp = {name: arr[l] for name, arr in stacked.items()}
        mod_p = (jax.nn.silu(c_ctx) @ lp['ada_w'] + lp['ada_b'])[None, :]
        y_prompt, ctx_new = trunk_layer(y_prompt, mod_p, lp, lbs[l], None)
        ks.append(ctx_new[0])
        vs.append(ctx_new[1])
        rgs.append(ctx_new[2])
        hgs.append(ctx_new[3])
        mod_s = jax.nn.silu(c) @ lp['ada_w'] + lp['ada_b']
        cached = (cache_k[:, l], cache_v[:, l], state_rglru[:, l], state_hgrn[:, l])
        y_sample, _ = trunk_layer(y_sample, mod_s, lp, lbs[l], cached)
    return (y_prompt, y_sample, jnp.stack(ks, axis=1), jnp.stack(vs, axis=1),
            jnp.stack(rgs, axis=1), jnp.stack(hgs, axis=1))
```

```python
import functools
import math

import jax
import jax.numpy as jnp
import numpy as np
from jax import lax
from jax.experimental import pallas as pl
from jax.experimental.pallas import tpu as pltpu

D_MODEL = 2048
DEPTH = 2
GRID_W = 64
GROUP_W = D_MODEL // 4
MIX_W = 4 * GROUP_W
HY_W = GROUP_W
HY_BANDS = 16
HY_SHORT = 3
RG_W = GROUP_W
RG_H = 4
RG_DH = RG_W // RG_H
RG_CONV = 4
RG_PAD = ((RG_CONV - 1) // 2, RG_CONV // 2)
RG_C = 8.0
HG_W = GROUP_W
HG_H = 4
HG_DK = HG_W // HG_H
HG_DV = HG_W // HG_H
HG_CHUNK = 32
ATT_W = GROUP_W
HEAD_DIM = 64
ATT_H = ATT_W // HEAD_DIM
ATT_KV = 2
ATT_G = ATT_H // ATT_KV
WINDOW = 128
ATT_BLOCK = 128
ATT_SCALE = 1.0 / math.sqrt(HEAD_DIM)
ROPE_BASE = 10000.0
NEG_INF = -1e30
D_FF = 5632
FFN_CONV = 3
IN_SIZES = (3 * HY_W, RG_W, RG_W, HG_W, HG_W, HG_W, HG_W, HG_W, ATT_W, ATT_KV * HEAD_DIM, ATT_KV * HEAD_DIM)
IN_COLS = sum(IN_SIZES)
NORM_EPS = 1e-6

V7X_VMEM_LIMIT_BYTES = 56 * 1024 * 1024
HALO_ROWS = 16
IN_COLS_PAD = 6144
IN_TILE_N = 512
FFN_TILE_F = 512
FFN_SUB_F = 256
ROW_TILE = 512
IN_ROW_TILE = 1024


def _compiler_params(semantics):
    return pltpu.CompilerParams(dimension_semantics=semantics, vmem_limit_bytes=V7X_VMEM_LIMIT_BYTES)


def _in_proj_kernel(x_ref, gs_ref, sh_ref, w_ref, z_ref, u_ref):
    @pl.when(pl.program_id(1) == 0)
    def _():
        x = x_ref[...]
        inv = lax.rsqrt(jnp.mean(x * x, axis=-1, keepdims=True) + NORM_EPS)
        u_ref[...] = (x * inv * gs_ref[0] + sh_ref[0]).astype(jnp.bfloat16)

    z_ref[...] = jnp.dot(u_ref[...], w_ref[...], preferred_element_type=jnp.float32)


def in_proj(x2d, gs, sh, w_bf16, seq_len, tm=IN_ROW_TILE, tn=IN_TILE_N):
    M, D = x2d.shape
    Np = w_bf16.shape[1]
    bm = gs.shape[0]
    mod_idx = (lambda i, j: (i * tm // seq_len, 0, 0)) if bm > 1 else (lambda i, j: (0, 0, 0))
    return pl.pallas_call(
        _in_proj_kernel,
        grid=(M // tm, Np // tn),
        in_specs=[
            pl.BlockSpec((tm, D), lambda i, j: (i, 0)),
            pl.BlockSpec((1, 1, D), mod_idx),
            pl.BlockSpec((1, 1, D), mod_idx),
            pl.BlockSpec((D, tn), lambda i, j: (0, j)),
        ],
        out_specs=pl.BlockSpec((tm, tn), lambda i, j: (i, j)),
        out_shape=jax.ShapeDtypeStruct((M, Np), jnp.float32),
        scratch_shapes=[pltpu.VMEM((tm, D), jnp.bfloat16)],
        compiler_params=_compiler_params(("parallel", "arbitrary")),
        name="in_proj",
    )(x2d, gs, sh, w_bf16)


def _out_proj_kernel(conv_ref, hx1_ref, base_ref, rg_ref, hg_ref, att_ref, w_ref, x_ref, g1_ref, gs2_ref, sh2_ref,
                     x1_ref, u2_ref):
    bf16 = jnp.bfloat16
    y_hy = hx1_ref[...] * conv_ref[...] + base_ref[...]
    m = jnp.dot(y_hy.astype(bf16), w_ref[0:GROUP_W, :], preferred_element_type=jnp.float32)
    for k, ref in enumerate((rg_ref, hg_ref, att_ref), start=1):
        m += jnp.dot(ref[...].astype(bf16), w_ref[k * GROUP_W:(k + 1) * GROUP_W, :], preferred_element_type=jnp.float32)
    x1 = x_ref[...] + g1_ref[0] * m
    x1_ref[...] = x1
    inv = lax.rsqrt(jnp.mean(x1 * x1, axis=-1, keepdims=True) + NORM_EPS)
    u2_ref[...] = (x1 * inv * gs2_ref[0] + sh2_ref[0]).astype(jnp.bfloat16)


def out_proj(mix_parts, w_bf16, x2d, g1, gs2, sh2, seq_len, tm=ROW_TILE):
    M, D = x2d.shape
    K = w_bf16.shape[0]
    bm = g1.shape[0]
    mod_idx = (lambda i: (i * tm // seq_len, 0, 0)) if bm > 1 else (lambda i: (0, 0, 0))
    mod_spec = pl.BlockSpec((1, 1, D), mod_idx)
    part_spec = pl.BlockSpec((tm, GROUP_W), lambda i: (i, 0))
    return pl.pallas_call(
        _out_proj_kernel,
        grid=(M // tm,),
        in_specs=[
            part_spec, part_spec, part_spec, part_spec, part_spec, part_spec,
            pl.BlockSpec((K, D), lambda i: (0, 0)),
            pl.BlockSpec((tm, D), lambda i: (i, 0)),
            mod_spec, mod_spec, mod_spec,
        ],
        out_specs=[pl.BlockSpec((tm, D), lambda i: (i, 0)), pl.BlockSpec((tm, D), lambda i: (i, 0))],
        out_shape=[jax.ShapeDtypeStruct((M, D), jnp.float32), jax.ShapeDtypeStruct((M, D), jnp.bfloat16)],
        compiler_params=_compiler_params(("parallel",)),
        name="out_proj",
    )(*mix_parts, w_bf16, x2d, g1, gs2, sh2)


def _conv_ffn_kernel(u_ref, up_ref, un_ref, wv_ref, wg_ref, cwv_ref, cwg_ref, cbv_ref, cbg_ref,
                     dn_ref, x1_ref, g2_ref, o_ref, ux_s, *, tm, seq_len):
    i = pl.program_id(0)
    j = pl.program_id(1)

    @pl.when(j == 0)
    def _():
        ux_s[0:HALO_ROWS, :] = up_ref[...]
        ux_s[HALO_ROWS:HALO_ROWS + tm, :] = u_ref[...]
        ux_s[HALO_ROWS + tm:, :] = un_ref[...]

    ux = ux_s[...]
    row = lax.broadcasted_iota(jnp.int32, (tm, 1), 0)
    pos = (row + i * tm) % seq_len
    interior_ends = seq_len < tm
    first_has_prev = ((i * tm) % seq_len) != 0
    last_has_next = ((i * tm + tm - 1) % seq_len) != seq_len - 1

    def branch(w_ref, cw_ref, cb_ref, cols):
        hx = jnp.dot(ux, w_ref[:, cols], preferred_element_type=jnp.float32)
        h = hx[HALO_ROWS:HALO_ROWS + tm]
        h_before = jnp.where(first_has_prev, hx[HALO_ROWS - 1:HALO_ROWS], 0.0)
        h_after = jnp.where(last_has_next, hx[HALO_ROWS + tm:HALO_ROWS + tm + 1], 0.0)
        h_m1 = jnp.where(row == 0, h_before, pltpu.roll(h, 1, axis=0))
        h_p1 = jnp.where(row == tm - 1, h_after, pltpu.roll(h, tm - 1, axis=0))
        if interior_ends:
            h_m1 = jnp.where(pos != 0, h_m1, 0.0)
            h_p1 = jnp.where(pos != seq_len - 1, h_p1, 0.0)
        cw = cw_ref[:, cols]
        return cw[0:1] * h_m1 + cw[1:2] * h + cw[2:3] * h_p1 + cb_ref[:, cols]

    tf = wv_ref.shape[1]
    acts = []
    for s0 in range(0, tf, FFN_SUB_F):
        cols = slice(s0, s0 + FFN_SUB_F)
        val = branch(wv_ref, cwv_ref, cbv_ref, cols)
        gate = branch(wg_ref, cwg_ref, cbg_ref, cols)
        acts.append((gate * jax.nn.sigmoid(gate) * val).astype(jnp.bfloat16))
    part = jnp.dot(jnp.concatenate(acts, axis=1), dn_ref[...], preferred_element_type=jnp.float32)

    @pl.when(j == 0)
    def _():
        o_ref[...] = part

    @pl.when(j > 0)
    def _():
        o_ref[...] += part

    @pl.when(j == pl.num_programs(1) - 1)
    def _():
        o_ref[...] = x1_ref[...] + g2_ref[0] * o_ref[...]


def conv_ffn(u2, up_bf16, conv_w, conv_b, down_bf16, x1, g2, seq_len, tm=ROW_TILE, tf=FFN_TILE_F):
    M, D = x1.shape
    F = down_bf16.shape[0]
    nf = F // tf
    bm = g2.shape[0]
    hb = tm // HALO_ROWS
    n_halo = M // HALO_ROWS
    mod_idx = (lambda i, j: (i * tm // seq_len, 0, 0)) if bm > 1 else (lambda i, j: (0, 0, 0))
    conv_b2 = conv_b.reshape(1, 2 * F)
    kern = functools.partial(_conv_ffn_kernel, tm=tm, seq_len=seq_len)
    return pl.pallas_call(
        kern,
        grid=(M // tm, nf),
        in_specs=[
            pl.BlockSpec((tm, D), lambda i, j: (i, 0)),
            pl.BlockSpec((HALO_ROWS, D), lambda i, j: (jnp.maximum(i * hb - 1, 0), 0)),
            pl.BlockSpec((HALO_ROWS, D), lambda i, j: (jnp.minimum((i + 1) * hb, n_halo - 1), 0)),
            pl.BlockSpec((D, tf), lambda i, j: (0, j)),
            pl.BlockSpec((D, tf), lambda i, j: (0, j + nf)),
            pl.BlockSpec((FFN_CONV, tf), lambda i, j: (0, j)),
            pl.BlockSpec((FFN_CONV, tf), lambda i, j: (0, j + nf)),
            pl.BlockSpec((1, tf), lambda i, j: (0, j)),
            pl.BlockSpec((1, tf), lambda i, j: (0, j + nf)),
            pl.BlockSpec((tf, D), lambda i, j: (j, 0)),
            pl.BlockSpec((tm, D), lambda i, j: (i, 0)),
            pl.BlockSpec((1, 1, D), mod_idx),
        ],
        out_specs=pl.BlockSpec((tm, D), lambda i, j: (i, 0)),
        out_shape=jax.ShapeDtypeStruct((M, D), jnp.float32),
        scratch_shapes=[pltpu.VMEM((tm + 2 * HALO_ROWS, D), jnp.bfloat16)],
        compiler_params=_compiler_params(("parallel", "arbitrary")),
        name="conv_ffn",
    )(u2, u2, u2, up_bf16, up_bf16, conv_w, conv_w, conv_b2, conv_b2, down_bf16, x1, g2)


def _segment_scan(a, b, reverse):
    n = a.shape[0]
    row = lax.broadcasted_iota(jnp.int32, (n, 1), 0)
    s = 1
    while s < n:
        shift = n - s if reverse else s
        valid = (row < n - s) if reverse else (row >= s)
        a_sh = jnp.where(valid, pltpu.roll(a, shift, axis=0), 1.0)
        b_sh = jnp.where(valid, pltpu.roll(b, shift, axis=0), 0.0)
        b = b + a * b_sh
        a = a * a_sh
        s *= 2
    return a, b


def _rglru_kernel(x_ref, g_ref, cw_ref, cb_ref, w_ref, gb_ref, ca_ref, h0_ref, y_ref, hT_ref, rx_s, hf_s, hb_s,
                  *, seq_len, chunk):
    n_chunks = seq_len // chunk
    row = lax.broadcasted_iota(jnp.int32, (chunk, 1), 0)
    cw = cw_ref[...]
    cb = cb_ref[...]

    def conv_body(k, carry):
        c0 = pl.multiple_of(k * chunk, chunk)
        x = x_ref[pl.ds(c0, chunk), :]
        before = x_ref[pl.ds(pl.multiple_of(jnp.maximum(c0 - 8, 0), 8), 8), :]
        after = x_ref[pl.ds(pl.multiple_of(jnp.minimum(c0 + chunk, seq_len - 8), 8), 8), :]
        pos = row + c0
        x_m1 = jnp.where(row == 0, before[7:8], pltpu.roll(x, 1, axis=0))
        x_p1 = jnp.where(row == chunk - 1, after[0:1], pltpu.roll(x, chunk - 1, axis=0))
        x_p2 = jnp.where(row == chunk - 2, after[0:1],
                         jnp.where(row == chunk - 1, after[1:2], pltpu.roll(x, chunk - 2, axis=0)))
        x_m1 = jnp.where(pos >= 1, x_m1, 0.0)
        x_p1 = jnp.where(pos < seq_len - 1, x_p1, 0.0)
        x_p2 = jnp.where(pos < seq_len - 2, x_p2, 0.0)
        rx_s[pl.ds(c0, chunk), :] = cw[0:1] * x_m1 + cw[1:2] * x + cw[2:3] * x_p1 + cw[3:4] * x_p2 + cb
        return carry

    lax.fori_loop(0, n_chunks, conv_body, 0)

    def affine(c0, d):
        rx = rx_s[pl.ds(c0, chunk), :]
        w = w_ref[0][:, d * 2 * RG_DH:(d + 1) * 2 * RG_DH]
        gates = jnp.dot(rx.astype(jnp.bfloat16), w, preferred_element_type=jnp.float32)
        gates = gates + gb_ref[0][:, d * 2 * RG_DH:(d + 1) * 2 * RG_DH]
        r = jax.nn.sigmoid(gates[:, :RG_DH])
        i = jax.nn.sigmoid(gates[:, RG_DH:])
        a = jnp.exp(ca_ref[d:d + 1, :] * r)
        b = jnp.sqrt(1.0 - a * a) * (i * rx)
        return a, b

    def scan_body(k, carry):
        h_f, h_b = carry
        cf = pl.multiple_of(k * chunk, chunk)
        cbk = pl.multiple_of((n_chunks - 1 - k) * chunk, chunk)
        a_f, b_f = affine(cf, 0)
        a_b, b_b = affine(cbk, 1)
        a_f, b_f = _segment_scan(a_f, b_f, reverse=False)
        a_b, b_b = _segment_scan(a_b, b_b, reverse=True)
        hh_f = b_f + a_f * h_f
        hh_b = b_b + a_b * h_b
        hf_s[pl.ds(cf, chunk), :] = hh_f
        hb_s[pl.ds(cbk, chunk), :] = hh_b
        return hh_f[chunk - 1:chunk], hh_b[0:1]

    h_f, h_b = lax.fori_loop(0, n_chunks, scan_body, (h0_ref[0][0:1, :], h0_ref[0][1:2, :]))
    hT_ref[0] = jnp.concatenate([h_f, h_b], axis=0)

    def out_body(k, carry):
        c0 = pl.multiple_of(k * chunk, chunk)
        y_ref[pl.ds(c0, chunk), :] = ((hf_s[pl.ds(c0, chunk), :] + hb_s[pl.ds(c0, chunk), :])
                                      * jax.nn.gelu(g_ref[pl.ds(c0, chunk), :]))
        return carry

    lax.fori_loop(0, n_chunks, out_body, 0)


RG_X_COL = 3 * HY_W
RG_GATE_COL = RG_X_COL + RG_W


def rglru(z2d, lp, h0, seq_len, chunk=128):
    M = z2d.shape[0]
    B = M // seq_len
    f32 = jnp.float32
    gw = lp['rg_gate_w']
    w = jnp.transpose(gw, (2, 3, 0, 1, 4)).reshape(RG_H, RG_DH, 4 * RG_DH).astype(jnp.bfloat16)
    gb = jnp.transpose(lp['rg_gate_b'].reshape(2, 2, RG_H, RG_DH), (2, 0, 1, 3)).reshape(RG_H, 1, 4 * RG_DH)
    ca = -RG_C * jax.nn.softplus(-lp['rg_lambda'].astype(f32))
    xcol = RG_X_COL // RG_DH
    gcol = RG_GATE_COL // RG_DH
    kern = functools.partial(_rglru_kernel, seq_len=seq_len, chunk=chunk)
    return pl.pallas_call(
        kern,
        grid=(B, RG_H),
        in_specs=[
            pl.BlockSpec((seq_len, RG_DH), lambda b, h: (b, xcol + h)),
            pl.BlockSpec((seq_len, RG_DH), lambda b, h: (b, gcol + h)),
            pl.BlockSpec((RG_CONV, RG_DH), lambda b, h: (0, h)),
            pl.BlockSpec((1, RG_DH), lambda b, h: (0, h)),
            pl.BlockSpec((1, RG_DH, 4 * RG_DH), lambda b, h: (h, 0, 0)),
            pl.BlockSpec((1, 1, 4 * RG_DH), lambda b, h: (h, 0, 0)),
            pl.BlockSpec((2, RG_DH), lambda b, h: (0, h)),
            pl.BlockSpec((1, 2, RG_DH), lambda b, h: (b, 0, h)),
        ],
        out_specs=[
            pl.BlockSpec((seq_len, RG_DH), lambda b, h: (b, h)),
            pl.BlockSpec((1, 2, RG_DH), lambda b, h: (b, 0, h)),
        ],
        out_shape=[jax.ShapeDtypeStruct((M, RG_W), f32), jax.ShapeDtypeStruct((B, 2, RG_W), f32)],
        scratch_shapes=[pltpu.VMEM((seq_len, RG_DH), f32), pltpu.VMEM((seq_len, RG_DH), f32),
                        pltpu.VMEM((seq_len, RG_DH), f32)],
        compiler_params=_compiler_params(("parallel", "parallel")),
        name="rglru",
    )(z2d, z2d, lp['rg_conv_w'], lp['rg_conv_b'].reshape(1, RG_W), w, gb, ca, h0.astype(f32))


HG_BLOCK = 256


def _chunk_cumsum(x, reverse):
    n = x.shape[0]
    r = lax.broadcasted_iota(jnp.int32, (n, 1), 0) % HG_CHUNK
    s = 1
    while s < HG_CHUNK:
        shift = n - s if reverse else s
        valid = (r < HG_CHUNK - s) if reverse else (r >= s)
        x = x + jnp.where(valid, pltpu.roll(x, shift, axis=0), 0.0)
        s *= 2
    return x


def _hgrn_block(q, v, graw, lb, s_t, mask, reverse):
    bf16 = jnp.bfloat16
    rows = q.shape[0]
    nseg = rows // HG_CHUNK
    f = lb + (1.0 - lb) * jax.nn.sigmoid(graw)
    k = 1.0 - f
    bc = _chunk_cumsum(jnp.log(f), reverse)
    mid = HG_CHUNK // 2 - 1
    ref_row = HG_CHUNK - 1 - mid if reverse else mid
    tot_row = 0 if reverse else HG_CHUNK - 1
    qs, ks, kd, tots = [], [], [], []
    for j in range(nseg):
        sl = slice(j * HG_CHUNK, (j + 1) * HG_CHUNK)
        g = bc[sl]
        ref = g[ref_row:ref_row + 1]
        tot = g[tot_row:tot_row + 1]
        qs.append(q[sl] * jnp.exp(g - ref))
        ks.append(k[sl] * jnp.exp(ref - g))
        kd.append(k[sl] * jnp.exp(tot - g))
        tots.append(jnp.exp(tot))
    qs = jnp.concatenate(qs, axis=0).astype(bf16)
    ks = jnp.concatenate(ks, axis=0).astype(bf16)
    att = lax.dot_general(qs, ks, (((1,), (1,)), ((), ())), preferred_element_type=jnp.float32)
    att = jnp.where(mask, att, 0.0).astype(bf16)
    v16 = v.astype(bf16)
    o_intra = jnp.dot(att, v16, preferred_element_type=jnp.float32)
    qd = (q * jnp.exp(bc)).astype(bf16)
    outs = [None] * nseg
    for j in (range(nseg - 1, -1, -1) if reverse else range(nseg)):
        sl = slice(j * HG_CHUNK, (j + 1) * HG_CHUNK)
        inter = lax.dot_general(qd[sl], s_t.astype(bf16), (((1,), (1,)), ((), ())),
                                preferred_element_type=jnp.float32)
        outs[j] = o_intra[sl] + inter
        kv_t = lax.dot_general(v16[sl], kd[j].astype(bf16), (((0,), (0,)), ((), ())),
                               preferred_element_type=jnp.float32)
        s_t = tots[j] * s_t + kv_t
    return jnp.concatenate(outs, axis=0), s_t


def _hgrn_kernel(q_ref, v_ref, ff_ref, fb_ref, gate_ref, lb_ref, ng_ref, s0_ref, y_ref, sT_ref, of_s, ob_s,
                 *, seq_len, block):
    n_blocks = seq_len // block
    r = lax.broadcasted_iota(jnp.int32, (block, block), 0)
    c = lax.broadcasted_iota(jnp.int32, (block, block), 1)
    same = (r // HG_CHUNK) == (c // HG_CHUNK)
    mask_f = same & (c <= r)
    mask_b = same & (c >= r)

    def scan_body(i, carry):
        s_f, s_b = carry
        cf = pl.multiple_of(i * block, block)
        cb = pl.multiple_of((n_blocks - 1 - i) * block, block)
        o_f, s_f = _hgrn_block(q_ref[pl.ds(cf, block), :], v_ref[pl.ds(cf, block), :], ff_ref[pl.ds(cf, block), :],
                               lb_ref[0:1, :], s_f, mask_f, reverse=False)
        o_b, s_b = _hgrn_block(q_ref[pl.ds(cb, block), :], v_ref[pl.ds(cb, block), :], fb_ref[pl.ds(cb, block), :],
                               lb_ref[1:2, :], s_b, mask_b, reverse=True)
        of_s[pl.ds(cf, block), :] = o_f
        ob_s[pl.ds(cb, block), :] = o_b
        return s_f, s_b

    s_f, s_b = lax.fori_loop(0, n_blocks, scan_body, (s0_ref[0, 0, 0].T, s0_ref[0, 1, 0].T))
    sT_ref[0, 0, 0] = s_f.T
    sT_ref[0, 1, 0] = s_b.T

    def out_body(i, carry):
        c0 = pl.multiple_of(i * block, block)
        o = of_s[pl.ds(c0, block), :] + ob_s[pl.ds(c0, block), :]
        inv = lax.rsqrt(jnp.mean(o * o, axis=-1, keepdims=True) + NORM_EPS)
        gate = gate_ref[pl.ds(c0, block), :]
        y_ref[pl.ds(c0, block), :] = o * inv * ng_ref[...] * (gate * jax.nn.sigmoid(gate))
        return carry

    lax.fori_loop(0, n_blocks, out_body, 0)


HG_Q_COL = RG_GATE_COL + RG_W


def hgrn2(z2d, lp, lb, s0, seq_len):
    M = z2d.shape[0]
    B = M // seq_len
    f32 = jnp.float32
    block = min(HG_BLOCK, seq_len)
    col = HG_Q_COL // HG_DK
    kern = functools.partial(_hgrn_kernel, seq_len=seq_len, block=block)

    def zspec(part):
        return pl.BlockSpec((seq_len, HG_DK), lambda b, h: (b, col + part * HG_H + h))

    return pl.pallas_call(
        kern,
        grid=(B, HG_H),
        in_specs=[
            zspec(0), zspec(1), zspec(2), zspec(3), zspec(4),
            pl.BlockSpec((2, HG_DK), lambda b, h: (0, h)),
            pl.BlockSpec((1, HG_DV), lambda b, h: (0, h)),
            pl.BlockSpec((1, 2, 1, HG_DK, HG_DV), lambda b, h: (b, 0, h, 0, 0)),
        ],
        out_specs=[
            pl.BlockSpec((seq_len, HG_DV), lambda b, h: (b, h)),
            pl.BlockSpec((1, 2, 1, HG_DK, HG_DV), lambda b, h: (b, 0, h, 0, 0)),
        ],
        out_shape=[jax.ShapeDtypeStruct((M, HG_W), f32), jax.ShapeDtypeStruct((B, 2, HG_H, HG_DK, HG_DV), f32)],
        scratch_shapes=[pltpu.VMEM((seq_len, HG_DV), f32), pltpu.VMEM((seq_len, HG_DV), f32)],
        compiler_params=_compiler_params(("parallel", "parallel")),
        name="hgrn2",
    )(z2d, z2d, z2d, z2d, z2d, lb.astype(f32), lp['hg_norm_g'].reshape(1, HG_W), s0.astype(f32))


ATT_Q_COL = HG_Q_COL + 5 * HG_W
ATT_KV_COL = ATT_Q_COL + ATT_W
LANES = 128
ATT_TQ = 512


def _head_mean_square(x):
    ss = x * x
    hi = ss.astype(jnp.bfloat16)
    lo = (ss - hi.astype(jnp.float32)).astype(jnp.bfloat16)
    r = lax.broadcasted_iota(jnp.int32, (LANES, LANES), 0) // HEAD_DIM
    c = lax.broadcasted_iota(jnp.int32, (LANES, LANES), 1) // HEAD_DIM
    ones = jnp.where(r == c, 1.0, 0.0).astype(jnp.bfloat16)
    tot = jnp.dot(hi, ones, preferred_element_type=jnp.float32) + jnp.dot(lo, ones, preferred_element_type=jnp.float32)
    return tot * (1.0 / HEAD_DIM)


def _qk_norm_rope(x, gain, cos, sin):
    y = x * lax.rsqrt(_head_mean_square(x) + NORM_EPS) * gain
    if cos is None:
        return y
    lane = lax.broadcasted_iota(jnp.int32, (1, LANES), 1) % (HEAD_DIM // 2)
    quarter = HEAD_DIM // 4
    partner = jnp.where(lane < quarter, pltpu.roll(y, LANES - quarter, axis=1), pltpu.roll(y, quarter, axis=1))
    return y * cos + partner * sin


def _both_halves(x, g):
    lane = lax.broadcasted_iota(jnp.int32, (1, LANES), 1)
    keep = (lane < HEAD_DIM) if g == 0 else (lane >= HEAD_DIM)
    return jnp.where(keep, x, pltpu.roll(x, HEAD_DIM, axis=1))


def _attn_kernel(*refs, latent, tq, seq_len):
    bf16 = jnp.bfloat16
    if latent:
        (q_ref, kvm_ref, kvp_ref, kvn_ref, cosm_ref, sinm_ref, cosp_ref, sinp_ref, cosn_ref, sinn_ref,
         ck_ref, cv_ref, qg_ref, kg_ref, sink_ref, y_ref, q_s, k2_s, v2_s, ck2_s, cv2_s) = refs
    else:
        (q_ref, kvm_ref, qg_ref, kg_ref, sink_ref, y_ref, ka_ref, q_s, k2_s, v2_s) = refs
    i = pl.program_id(1)
    blk = ATT_BLOCK

    if latent:
        pieces = [(kvp_ref, cosp_ref, sinp_ref, 0, blk), (kvm_ref, cosm_ref, sinm_ref, blk, tq),
                  (kvn_ref, cosn_ref, sinn_ref, blk + tq, blk)]
    else:
        pieces = [(kvm_ref, None, None, 0, tq)]
    for kv_ref, cos_ref, sin_ref, r0, n in pieces:
        kv = kv_ref[...]
        kn = _qk_norm_rope(kv[:, :LANES], kg_ref[...], None if cos_ref is None else cos_ref[...],
                           None if sin_ref is None else sin_ref[...])
        if not latent:
            ka_ref[...] = kn
        for g in range(ATT_KV):
            k2_s[g, r0:r0 + n, :] = _both_halves(kn, g).astype(bf16)
            v2_s[g, r0:r0 + n, :] = _both_halves(kv[:, LANES:], g).astype(bf16)
    if latent:
        for g in range(ATT_KV):
            ck2_s[g] = _both_halves(ck_ref[0], g).astype(bf16)
            cv2_s[g] = _both_halves(cv_ref[0], g).astype(bf16)

    for p in range(ATT_W // LANES):
        qn = _qk_norm_rope(q_ref[:, p * LANES:(p + 1) * LANES], qg_ref[...],
                           cosm_ref[...] if latent else None, sinm_ref[...] if latent else None)
        q_s[:, p * LANES:(p + 1) * LANES] = (qn * ATT_SCALE).astype(bf16)

    rows = ATT_G * blk
    lane = lax.broadcasted_iota(jnp.int32, (1, LANES), 1)
    low = lane < HEAD_DIM
    head_of_row = lax.broadcasted_iota(jnp.int32, (rows, 1), 0) // blk
    nt = (((1,), (1,)), ((), ()))
    for j in range(tq // blk):
        for g in range(ATT_KV):
            stack = []
            for p in (2 * g, 2 * g + 1):
                qb = q_s[j * blk:(j + 1) * blk, p * LANES:(p + 1) * LANES]
                stack.append(jnp.where(low, qb, jnp.zeros_like(qb)))
                stack.append(jnp.where(low, jnp.zeros_like(qb), qb))
            qg = jnp.concatenate(stack, axis=0)
            sink = jnp.zeros((rows, 1), jnp.float32)
            for hh in range(ATT_G):
                sink = jnp.where(head_of_row == hh, sink_ref[ATT_G * g + hh:ATT_G * g + hh + 1, 0:1], sink)
            if latent:
                nk = 3 * blk
                s_w = lax.dot_general(qg, k2_s[g, j * blk:j * blk + nk, :], nt, preferred_element_type=jnp.float32)
                a = lax.broadcasted_iota(jnp.int32, (rows, nk), 0) % blk
                c = lax.broadcasted_iota(jnp.int32, (rows, nk), 1)
                kpos = c + (i * tq + j * blk - blk)
                ok = (jnp.abs(c - blk - a) <= WINDOW) & (kpos >= 0) & (kpos < seq_len)
                s_w = jnp.where(ok, s_w, NEG_INF)
                s_c = lax.dot_general(qg, ck2_s[g], nt, preferred_element_type=jnp.float32)
                m = jnp.maximum(jnp.maximum(jnp.max(s_w, axis=-1, keepdims=True),
                                            jnp.max(s_c, axis=-1, keepdims=True)), sink)
                e_w = jnp.exp(s_w - m)
                e_c = jnp.exp(s_c - m)
                den = jnp.sum(e_w, axis=-1, keepdims=True) + jnp.sum(e_c, axis=-1, keepdims=True) + jnp.exp(sink - m)
                o = (jnp.dot(e_w.astype(bf16), v2_s[g, j * blk:j * blk + nk, :], preferred_element_type=jnp.float32)
                     + jnp.dot(e_c.astype(bf16), cv2_s[g], preferred_element_type=jnp.float32))
            else:
                s = lax.dot_general(qg, k2_s[g], nt, preferred_element_type=jnp.float32)
                m = jnp.maximum(jnp.max(s, axis=-1, keepdims=True), sink)
                e = jnp.exp(s - m)
                den = jnp.sum(e, axis=-1, keepdims=True) + jnp.exp(sink - m)
                o = jnp.dot(e.astype(bf16), v2_s[g], preferred_element_type=jnp.float32)
            o = o / den
            for t, p in enumerate((2 * g, 2 * g + 1)):
                y_ref[j * blk:(j + 1) * blk, p * LANES:(p + 1) * LANES] = jnp.where(
                    low, o[2 * t * blk:(2 * t + 1) * blk], o[(2 * t + 1) * blk:(2 * t + 2) * blk])


def _rope_tables(seq_len):
    f32 = jnp.float32
    half = HEAD_DIM // 2
    t = jnp.arange(seq_len)
    row = (t // GRID_W).astype(f32)
    col = (t % GRID_W).astype(f32)
    inv = ROPE_BASE ** (-jnp.arange(0, half, 2, dtype=f32) / half)

    def tab(pos):
        ang = pos[:, None] * inv[None, :]
        return (jnp.concatenate([jnp.cos(ang), jnp.cos(ang)], axis=-1),
                jnp.concatenate([-jnp.sin(ang), jnp.sin(ang)], axis=-1))

    cr, sr = tab(row)
    cc, sc = tab(col)
    cos = jnp.concatenate([cr, cc], axis=-1)
    sin = jnp.concatenate([sr, sc], axis=-1)
    return jnp.tile(cos, (1, 2)), jnp.tile(sin, (1, 2))


def attention(z2d, lp, seq_len, ctx_k=None, ctx_v=None):
    M = z2d.shape[0]
    B = M // seq_len
    f32 = jnp.float32
    bf16 = jnp.bfloat16
    latent = ctx_k is not None
    tq = min(ATT_TQ, seq_len)
    nq = seq_len // tq
    qg = jnp.tile(lp['att_qn_g'].astype(f32), 2).reshape(1, LANES)
    kg = jnp.tile(lp['att_kn_g'].astype(f32), 2).reshape(1, LANES)
    sink = jnp.broadcast_to(lp['att_sink'].astype(f32)[:, None], (ATT_H, LANES))
    qcol = ATT_Q_COL // ATT_W
    kvcol = ATT_KV_COL // (2 * LANES)
    kern = functools.partial(_attn_kernel, latent=latent, tq=tq, seq_len=seq_len)
    q_spec = pl.BlockSpec((tq, ATT_W), lambda b, i: (b * nq + i, qcol))
    kv_spec = pl.BlockSpec((tq, 2 * LANES), lambda b, i: (b * nq + i, kvcol))
    small = lambda shape: pl.BlockSpec(shape, lambda b, i: (0, 0))
    y_spec = pl.BlockSpec((tq, ATT_W), lambda b, i: (b * nq + i, 0))
    params = _compiler_params(("parallel", "parallel"))
    if not latent:
        return pl.pallas_call(
            kern, grid=(B, nq),
            in_specs=[q_spec, kv_spec, small((1, LANES)), small((1, LANES)), small((ATT_H, LANES))],
            out_specs=[y_spec, pl.BlockSpec((tq, LANES), lambda b, i: (b * nq + i, 0))],
            out_shape=[jax.ShapeDtypeStruct((M, ATT_W), f32), jax.ShapeDtypeStruct((M, LANES), f32)],
            scratch_shapes=[pltpu.VMEM((tq, ATT_W), bf16), pltpu.VMEM((ATT_KV, tq, LANES), bf16),
                            pltpu.VMEM((ATT_KV, tq, LANES), bf16)],
            compiler_params=params, name="attn_context",
        )(z2d, z2d, qg, kg, sink)
    P = ctx_k.shape[1]
    hb = tq // ATT_BLOCK
    nblk = seq_len // ATT_BLOCK
    cos, sin = _rope_tables(seq_len)
    prev_idx = lambda b, i: (b * nblk + jnp.maximum(i * hb - 1, 0), kvcol)
    next_idx = lambda b, i: (b * nblk + jnp.minimum((i + 1) * hb, nblk - 1), kvcol)
    tab_m = pl.BlockSpec((tq, LANES), lambda b, i: (i, 0))
    tab_p = pl.BlockSpec((ATT_BLOCK, LANES), lambda b, i: (jnp.maximum(i * hb - 1, 0), 0))
    tab_n = pl.BlockSpec((ATT_BLOCK, LANES), lambda b, i: (jnp.minimum((i + 1) * hb, nblk - 1), 0))
    ctx_spec = pl.BlockSpec((1, P, LANES), lambda b, i: (b, 0, 0))
    nk_all = tq + 2 * ATT_BLOCK
    return pl.pallas_call(
        kern, grid=(B, nq),
        in_specs=[q_spec, kv_spec, pl.BlockSpec((ATT_BLOCK, 2 * LANES), prev_idx),
                  pl.BlockSpec((ATT_BLOCK, 2 * LANES), next_idx),
                  tab_m, tab_m, tab_p, tab_p, tab_n, tab_n, ctx_spec, ctx_spec,
                  small((1, LANES)), small((1, LANES)), small((ATT_H, LANES))],
        out_specs=y_spec,
        out_shape=jax.ShapeDtypeStruct((M, ATT_W), f32),
        scratch_shapes=[pltpu.VMEM((tq, ATT_W), bf16), pltpu.VMEM((ATT_KV, nk_all, LANES), bf16),
                        pltpu.VMEM((ATT_KV, nk_all, LANES), bf16), pltpu.VMEM((ATT_KV, P, LANES), bf16),
                        pltpu.VMEM((ATT_KV, P, LANES), bf16)],
        compiler_params=params, name="attn_latent",
    )(z2d, z2d, z2d, z2d, cos, sin, cos, sin, cos, sin, ctx_k.astype(f32), ctx_v.astype(f32), qg, kg, sink)


HY_CHUNK = 256
HY_TILE_F = 256
DFT_SPLIT = 64


def _hy_pre_kernel(hv_ref, h1_ref, h2_ref, cw_ref, cb_ref, skip_ref, w_ref, x1_ref, base_ref, *, seq_len, chunk):
    n_chunks = seq_len // chunk
    row = lax.broadcasted_iota(jnp.int32, (chunk, 1), 0)

    def conv(ref, part, c0):
        x = ref[pl.ds(c0, chunk), :]
        before = ref[pl.ds(pl.multiple_of(jnp.maximum(c0 - 8, 0), 8), 8), :]
        after = ref[pl.ds(pl.multiple_of(jnp.minimum(c0 + chunk, seq_len - 8), 8), 8), :]
        pos = row + c0
        x_m1 = jnp.where(row == 0, before[7:8], pltpu.roll(x, 1, axis=0))
        x_p1 = jnp.where(row == chunk - 1, after[0:1], pltpu.roll(x, chunk - 1, axis=0))
        x_m1 = jnp.where(pos >= 1, x_m1, 0.0)
        x_p1 = jnp.where(pos < seq_len - 1, x_p1, 0.0)
        cw = cw_ref[part]
        return cw[0:1] * x_m1 + cw[1:2] * x + cw[2:3] * x_p1 + cb_ref[part]

    def body(k, carry):
        c0 = pl.multiple_of(k * chunk, chunk)
        hv = conv(hv_ref, 0, c0)
        hx1 = conv(h1_ref, 1, c0)
        hx2 = conv(h2_ref, 2, c0)
        w = hv * hx2
        w_ref[pl.ds(c0, chunk), :] = w.astype(jnp.bfloat16)
        x1_ref[pl.ds(c0, chunk), :] = hx1
        base_ref[pl.ds(c0, chunk), :] = hx1 * (w * skip_ref[...])
        return carry

    lax.fori_loop(0, n_chunks, body, 0)


def hyena_pre(z2d, lp, seq_len):
    M = z2d.shape[0]
    B = M // seq_len
    f32 = jnp.float32
    nblk = HY_W // LANES
    chunk = min(HY_CHUNK, seq_len)
    cw = lp['hy_conv_w'].reshape(HY_SHORT, 3, HY_W).transpose(1, 0, 2)
    cb = lp['hy_conv_b'].reshape(3, 1, HY_W)
    kern = functools.partial(_hy_pre_kernel, seq_len=seq_len, chunk=chunk)
    zspec = lambda part: pl.BlockSpec((seq_len, LANES), lambda b, c: (b, part * nblk + c))
    ospec = pl.BlockSpec((seq_len, LANES), lambda b, c: (b, c))
    return pl.pallas_call(
        kern, grid=(B, nblk),
        in_specs=[zspec(0), zspec(1), zspec(2),
                  pl.BlockSpec((3, HY_SHORT, LANES), lambda b, c: (0, 0, c)),
                  pl.BlockSpec((3, 1, LANES), lambda b, c: (0, 0, c)),
                  pl.BlockSpec((1, LANES), lambda b, c: (0, c))],
        out_specs=[ospec, ospec, ospec],
        out_shape=[jax.ShapeDtypeStruct((M, HY_W), jnp.bfloat16), jax.ShapeDtypeStruct((M, HY_W), f32),
                   jax.ShapeDtypeStruct((M, HY_W), f32)],
        compiler_params=_compiler_params(("parallel", "parallel")),
        name="hyena_pre",
    )(z2d, z2d, z2d, cw, cb, lp['hy_skip'].reshape(1, HY_W).astype(f32))


def _hy_conv_kernel(g_ref, gi_ref, w_ref, kra_ref, krb_ref, ki_ref, o_ref, *, tf, seq_len):
    j = pl.program_id(1)
    x = jnp.dot(g_ref[0], w_ref[...], preferred_element_type=jnp.float32)
    xr, xs = x[:tf], x[tf:]
    ki = ki_ref[...]
    a = xr * kra_ref[...] + xs * ki
    b = xs * krb_ref[...] - xr * ki
    ab = (jnp.concatenate([a, b], axis=0) * (1.0 / seq_len)).astype(jnp.bfloat16)
    part = jnp.dot(gi_ref[0], ab, preferred_element_type=jnp.float32)

    @pl.when(j == 0)
    def _():
        o_ref[...] = part

    @pl.when(j > 0)
    def _():
        o_ref[...] += part


def _dft_tables(seq_len, tf):
    n = 2 * seq_len
    nf = seq_len // tf
    f = jnp.arange(seq_len, dtype=jnp.int32)

    def small(freqs):
        ang = ((freqs[:, None] * f[None, :]) % n).astype(jnp.float32) * (2.0 * math.pi / n)
        return jnp.cos(ang), jnp.sin(ang)

    c1, s1 = small(DFT_SPLIT * jnp.arange(seq_len // DFT_SPLIT, dtype=jnp.int32))
    c0, s0 = small(jnp.arange(DFT_SPLIT, dtype=jnp.int32))
    c = (c1[:, None, :] * c0[None, :, :] - s1[:, None, :] * s0[None, :, :]).reshape(seq_len, seq_len)
    s = (s1[:, None, :] * c0[None, :, :] + c1[:, None, :] * s0[None, :, :]).reshape(seq_len, seq_len)
    alt = (1 - 2 * (f % 2)).astype(jnp.float32)
    s = jnp.where(f[:, None] == 0, alt[None, :], s)
    fwd = jnp.concatenate([c.reshape(nf, tf, seq_len), s.reshape(nf, tf, seq_len)], axis=1)
    inv = jnp.concatenate([c.reshape(seq_len, nf, tf), s.T.reshape(seq_len, nf, tf)], axis=2).transpose(1, 0, 2)
    return fwd.astype(jnp.bfloat16), inv.astype(jnp.bfloat16)


def hyena_long_conv(w_bf16, k_long, seq_len):
    M, W = w_bf16.shape
    B = M // seq_len
    f32 = jnp.float32
    tf = min(HY_TILE_F, seq_len)
    nf = seq_len // tf
    fwd, inv = _dft_tables(seq_len, tf)
    kf = jnp.fft.rfft(k_long, axis=0)
    kr = jnp.real(kf[:seq_len]).astype(f32)
    ki = jnp.imag(kf[:seq_len]).astype(f32).at[0].set(0.0)
    kra = kr.at[0].multiply(0.5)
    krb = kr.at[0].set(0.5 * jnp.real(kf[seq_len]).astype(f32))
    kern = functools.partial(_hy_conv_kernel, tf=tf, seq_len=seq_len)
    kspec = pl.BlockSpec((tf, W), lambda b, j: (j, 0))
    return pl.pallas_call(
        kern, grid=(B, nf),
        in_specs=[pl.BlockSpec((1, 2 * tf, seq_len), lambda b, j: (j, 0, 0)),
                  pl.BlockSpec((1, seq_len, 2 * tf), lambda b, j: (j, 0, 0)),
                  pl.BlockSpec((seq_len, W), lambda b, j: (b, 0)),
                  kspec, kspec, kspec],
        out_specs=pl.BlockSpec((seq_len, W), lambda b, j: (b, 0)),
        out_shape=jax.ShapeDtypeStruct((M, W), f32),
        compiler_params=_compiler_params(("parallel", "arbitrary")),
        name="hyena_long_conv",
    )(fwd, inv, w_bf16, kra, krb, ki)


def rms_norm(x, g, eps=NORM_EPS):
    xf = x.astype(jnp.float32)
    y = xf * lax.rsqrt(jnp.mean(xf * xf, axis=-1, keepdims=True) + eps)
    return (y * g.astype(jnp.float32)).astype(x.dtype)


def dwconv(x, w, b, pad):
    y = lax.conv_general_dilated(x, w.astype(x.dtype)[:, None, :], window_strides=(1,), padding=[pad],
                                 dimension_numbers=('NWC', 'WIO', 'NWC'), feature_group_count=x.shape[-1])
    return y + b.astype(x.dtype)


def hyena_kernel(L, w1, b1, freq, w2, b2, w3, b3, decay):
    f32 = jnp.float32
    t_idx = jnp.arange(L, dtype=f32)
    t = t_idx / (L - 1)
    ang = (2.0 * math.pi * t_idx / L)[:, None] * jnp.linspace(1e-4, HY_BANDS - 1, HY_BANDS, dtype=f32)[None, :]
    z = jnp.concatenate([t[:, None], jnp.cos(ang), -jnp.sin(ang)], axis=-1)
    h = jnp.sin(freq[0].astype(f32) * (z @ w1.astype(f32) + b1.astype(f32)))
    h = jnp.sin(freq[1].astype(f32) * (h @ w2.astype(f32) + b2.astype(f32)))
    h = (h @ w3.astype(f32) + b3.astype(f32)) * jnp.exp(-t[:, None] * jnp.abs(decay.astype(f32)))
    h_fwd, h_bwd = h[:, :HY_W], h[:, HY_W:]
    k = jnp.concatenate([h_fwd, jnp.zeros((1, HY_W), f32), h_bwd[:0:-1]], axis=0)
    return k / jnp.sum(jnp.abs(k), axis=0, keepdims=True)


def centred_long_conv(u, k, skip):
    L = u.shape[1]
    spec = jnp.fft.rfft(u, n=2 * L, axis=1) * jnp.fft.rfft(k, axis=0)[None]
    return jnp.fft.irfft(spec, n=2 * L, axis=1)[:, :L] + u * skip


def _affine_combine(e1, e2):
    a1, b1 = e1
    a2, b2 = e2
    return a1 * a2, a2 * b1 + b2


def rglru_scan(x, gate_w, gate_b, lam, h0):
    f32 = jnp.float32
    B, T, W = x.shape
    xh = x.reshape(B, T, RG_H, RG_DH)
    gates = jnp.einsum('bthi,ghij->gbthj', xh, gate_w.astype(f32)).reshape(2, B, T, W) + gate_b.astype(f32)[:, None, None, :]
    r = jax.nn.sigmoid(gates[0])
    i = jax.nn.sigmoid(gates[1])
    log_a = -RG_C * r * jax.nn.softplus(-lam.astype(f32))
    a = jnp.exp(log_a)
    b = jnp.sqrt(-jnp.expm1(2.0 * log_a)) * (i * x)
    b = b.at[:, 0].add(a[:, 0] * h0)
    _, h = lax.associative_scan(_affine_combine, (a, b), axis=1)
    return h, h[:, -1]


def hgrn2_scan(q, k, v, log_f, S0):
    B, T, H, Dk = q.shape
    Dv = v.shape[-1]
    N = T // HG_CHUNK

    def chunks(a):
        return a.reshape(B, N, HG_CHUNK, H, a.shape[-1]).transpose(1, 0, 3, 2, 4)

    causal = jnp.tril(jnp.ones((HG_CHUNK, HG_CHUNK), jnp.float32))
    mid = HG_CHUNK // 2 - 1

    def step(S, inp):
        qc, kc, vc, gc = inp
        bc = jnp.cumsum(gc, axis=2)
        ref = bc[:, :, mid:mid + 1]
        att = jnp.einsum('bhtd,bhsd->bhts', qc * jnp.exp(bc - ref), kc * jnp.exp(ref - bc)) * causal
        o = jnp.einsum('bhts,bhse->bhte', att, vc) + jnp.einsum('bhtd,bhde->bhte', qc * jnp.exp(bc), S)
        bl = bc[:, :, -1:]
        S = jnp.exp(bl)[:, :, 0, :, None] * S + jnp.einsum('bhsd,bhse->bhde', kc * jnp.exp(bl - bc), vc)
        return S, o

    S, o = lax.scan(step, S0, (chunks(q), chunks(k), chunks(v), chunks(log_f)))
    return o.transpose(1, 0, 3, 2, 4).reshape(B, T, H, Dv), S


def axial_rope(x):
    f32 = jnp.float32
    T = x.shape[1]
    rows = T // GRID_W
    row = jnp.repeat(jnp.arange(rows, dtype=f32), GRID_W)
    col = jnp.tile(jnp.arange(GRID_W, dtype=f32), rows)
    half = HEAD_DIM // 2
    inv = ROPE_BASE ** (-jnp.arange(0, half, 2, dtype=f32) / half)

    def rot(xa, pos):
        ang = pos[:, None] * inv[None, :]
        cos = jnp.cos(ang)[None, :, None, :]
        sin = jnp.sin(ang)[None, :, None, :]
        x1, x2 = jnp.split(xa.astype(f32), 2, axis=-1)
        return jnp.concatenate([x1 * cos - x2 * sin, x1 * sin + x2 * cos], axis=-1)

    return jnp.concatenate([rot(x[..., :half], row), rot(x[..., half:], col)], axis=-1).astype(x.dtype)


def sink_attend(qb, keys, vals, bias, sink):
    s = jnp.einsum('bqkgd,bskd->bkgqs', qb, keys).astype(jnp.float32) * ATT_SCALE + bias
    sk = sink[None, :, :, None, None]
    m = jnp.maximum(jnp.max(s, axis=-1, keepdims=True), sk)
    e = jnp.exp(s - m)
    p = e / (jnp.sum(e, axis=-1, keepdims=True) + jnp.exp(sk - m))
    return jnp.einsum('bkgqs,bskd->bqkgd', p.astype(vals.dtype), vals)


def attn_context(q, k, v, sink):
    B, S = q.shape[:2]
    nb = S // ATT_BLOCK
    qb = q.reshape(B, nb, ATT_BLOCK, ATT_KV, ATT_G, HEAD_DIM).swapaxes(0, 1)
    o = lax.map(lambda qq: sink_attend(qq, k, v, 0.0, sink), qb)
    return o.swapaxes(0, 1).reshape(B, S, ATT_W)


def attn_latent(q, k, v, kc, vc, sink):
    B, T = q.shape[:2]
    nb = T // ATT_BLOCK
    P = kc.shape[1]

    def blocks3(a):
        ap = jnp.pad(a, ((0, 0), (ATT_BLOCK, ATT_BLOCK), (0, 0), (0, 0))).reshape(B, nb + 2, ATT_BLOCK, ATT_KV, HEAD_DIM)
        return jnp.concatenate([ap[:, :-2], ap[:, 1:-1], ap[:, 2:]], axis=2).swapaxes(0, 1)

    kl, vl = blocks3(k), blocks3(v)
    qb = q.reshape(B, nb, ATT_BLOCK, ATT_KV, ATT_G, HEAD_DIM).swapaxes(0, 1)
    base = jnp.arange(nb)[:, None, None] * ATT_BLOCK
    qpos = base + jnp.arange(ATT_BLOCK)[None, :, None]
    kpos = base - ATT_BLOCK + jnp.arange(3 * ATT_BLOCK)[None, None, :]
    ok = (jnp.abs(qpos - kpos) <= WINDOW) & (kpos >= 0) & (kpos < T)
    bias = jnp.concatenate([jnp.where(ok, 0.0, NEG_INF).astype(jnp.float32),
                            jnp.zeros((nb, ATT_BLOCK, P), jnp.float32)], axis=-1)

    def one(args):
        qq, kk, vv, bb = args
        return sink_attend(qq, jnp.concatenate([kk, kc], axis=1), jnp.concatenate([vv, vc], axis=1), bb, sink)

    o = lax.map(one, (qb, kl, vl, bias))
    return o.swapaxes(0, 1).reshape(B, T, ATT_W)


def token_mixers(z, lp, lb, ctx):
    f32 = jnp.float32
    B, T, _ = z.shape
    z2d = z.reshape(B * T, -1)

    k_long = hyena_kernel(T, lp['hy_f_w1'], lp['hy_f_b1'], lp['hy_f_freq'], lp['hy_f_w2'], lp['hy_f_b2'],
                          lp['hy_f_w3'], lp['hy_f_b3'], lp['hy_decay'])
    hy_w, hy_x1, hy_base = hyena_pre(z2d, lp, T)
    hy_conv = hyena_long_conv(hy_w, k_long, T)

    h0 = jnp.zeros((B, 2, RG_W), f32) if ctx is None else ctx[2]
    y_rg, hT_rg = rglru(z2d, lp, h0, T)

    S0 = jnp.zeros((B, 2, HG_H, HG_DK, HG_DV), f32) if ctx is None else ctx[3]
    y_hg, S_hg = hgrn2(z2d, lp, lb, S0, T)

    if ctx is None:
        y_att, ka = attention(z2d, lp, T)
        ka = ka.reshape(B, T, ATT_KV, HEAD_DIM)
        va = z[..., ATT_KV_COL + LANES:ATT_KV_COL + 2 * LANES].reshape(B, T, ATT_KV, HEAD_DIM)
        new_ctx = (ka, va, hT_rg, S_hg)
    else:
        P = ctx[0].shape[1]
        y_att = attention(z2d, lp, T, ctx[0].reshape(B, P, ATT_KV * HEAD_DIM), ctx[1].reshape(B, P, ATT_KV * HEAD_DIM))
        new_ctx = None
    return (hy_conv, hy_x1, hy_base, y_rg, y_hg, y_att), new_ctx


def trunk_layer(x, mod, lp, lb, ctx):
    B, T, D = x.shape
    sh1, sc1, g1, sh2, sc2, g2 = [m[:, None, :] for m in jnp.split(mod, 6, axis=-1)]
    gs1 = lp['norm1_g'][None, None, :] * (1.0 + sc1)
    gs2 = lp['norm2_g'][None, None, :] * (1.0 + sc2)
    x2d = x.reshape(B * T, D)
    z = in_proj(x2d, gs1, sh1, lp['w_in_bf16'], T)
    mix_parts, new_ctx = token_mixers(z.reshape(B, T, IN_COLS_PAD), lp, lb, ctx)
    x1, u2 = out_proj(mix_parts, lp['w_out_bf16'], x2d, g1, gs2, sh2, T)
    x2 = conv_ffn(u2, lp['ffn_up_bf16'], lp['ffn_conv_w'], lp['ffn_conv_b'], lp['ffn_down_bf16'], x1, g2, T)
    return x2.reshape(B, T, D), new_ctx


def kernel(x_prompt, x_sample, cache_k, cache_v, state_rglru, state_hgrn, c, c_ctx, norm1_g, norm2_g, ada_w, ada_b, w_in, w_out, hy_conv_w, hy_conv_b, hy_f_w1, hy_f_b1, hy_f_freq, hy_f_w2, hy_f_b2, hy_f_w3, hy_f_b3, hy_decay, hy_skip, rg_conv_w, rg_conv_b, rg_gate_w, rg_gate_b, rg_lambda, hg_lb, hg_norm_g, att_qn_g, att_kn_g, att_sink, ffn_up, ffn_conv_w, ffn_conv_b, ffn_down):
    bf16 = jnp.bfloat16
    stacked = dict(norm1_g=norm1_g, norm2_g=norm2_g, ada_w=ada_w, ada_b=ada_b,
                   hy_conv_w=hy_conv_w, hy_conv_b=hy_conv_b, hy_f_w1=hy_f_w1, hy_f_b1=hy_f_b1, hy_f_freq=hy_f_freq,
                   hy_f_w2=hy_f_w2, hy_f_b2=hy_f_b2, hy_f_w3=hy_f_w3, hy_f_b3=hy_f_b3, hy_decay=hy_decay, hy_skip=hy_skip,
                   rg_conv_w=rg_conv_w, rg_conv_b=rg_conv_b, rg_gate_w=rg_gate_w, rg_gate_b=rg_gate_b, rg_lambda=rg_lambda,
                   hg_norm_g=hg_norm_g, att_qn_g=att_qn_g, att_kn_g=att_kn_g, att_sink=att_sink,
                   ffn_conv_w=ffn_conv_w, ffn_conv_b=ffn_conv_b,
                   w_in_bf16=jnp.pad(w_in.astype(bf16), ((0, 0), (0, 0), (0, IN_COLS_PAD - IN_COLS))),
                   w_out_bf16=w_out.astype(bf16), ffn_up_bf16=ffn_up.astype(bf16), ffn_down_bf16=ffn_down.astype(bf16))
    lbs = jnp.cumsum(jax.nn.softmax(hg_lb.astype(jnp.float32), axis=0), axis=0)
    lbs = lbs - lbs[0:1]
    y_prompt, y_sample = x_prompt, x_sample
    ks, vs, rgs, hgs = [], [], [], []
    for l in range(DEPTH):
        lp = {name: arr[l] for name, arr in stacked.items()}
        mod_p = (jax.nn.silu(c_ctx) @ lp['ada_w'] + lp['ada_b'])[None, :]
        y_prompt, ctx_new = trunk_layer(y_prompt, mod_p, lp, lbs[l], None)
        ks.append(ctx_new[0])
        vs.append(ctx_new[1])
        rgs.append(ctx_new[2])
        hgs.append(ctx_new[3])
        mod_s = jax.nn.silu(c) @ lp['ada_w'] + lp['ada_b']
        cached = (cache_k[:, l], cache_v[:, l], state_rglru[:, l], state_hgrn[:, l])
        y_sample, _ = trunk_layer(y_sample, mod_s, lp, lbs[l], cached)
    return (y_prompt, y_sample, jnp.stack(ks, axis=1), jnp.stack(vs, axis=1),
            jnp.stack(rgs, axis=1), jnp.stack(hgs, axis=1))
```

```python
import functools
import math

import jax
import jax.numpy as jnp
from jax import lax
from jax.experimental import pallas as pl
from jax.experimental.pallas import tpu as pltpu

D_MODEL = 2048
DEPTH = 2
GRID_W = 64
GROUP_W = D_MODEL // 4
MIX_W = 4 * GROUP_W
HY_W = GROUP_W
HY_BANDS = 16
HY_SHORT = 3
RG_W = GROUP_W
RG_H = 4
RG_DH = RG_W // RG_H
RG_CONV = 4
RG_PAD = ((RG_CONV - 1) // 2, RG_CONV // 2)
RG_C = 8.0
HG_W = GROUP_W
HG_H = 4
HG_DK = HG_W // HG_H
HG_DV = HG_W // HG_H
HG_CHUNK = 32
ATT_W = GROUP_W
HEAD_DIM = 64
ATT_H = ATT_W // HEAD_DIM
ATT_KV = 2
ATT_G = ATT_H // ATT_KV
WINDOW = 128
ATT_BLOCK = 128
ATT_SCALE = 1.0 / math.sqrt(HEAD_DIM)
ROPE_BASE = 10000.0
NEG_INF = -1e30
D_FF = 5632
FFN_CONV = 3
IN_SIZES = (3 * HY_W, RG_W, RG_W, HG_W, HG_W, HG_W, HG_W, HG_W, ATT_W, ATT_KV * HEAD_DIM, ATT_KV * HEAD_DIM)
IN_COLS = sum(IN_SIZES)
NORM_EPS = 1e-6

V7X_VMEM_LIMIT_BYTES = 56 * 1024 * 1024
HALO_ROWS = 16
IN_COLS_PAD = 6144
IN_TILE_N = 512
FFN_TILE_F = 512
FFN_SUB_F = 256
ROW_TILE = 512
IN_ROW_TILE = 1024


def _compiler_params(semantics):
    return pltpu.CompilerParams(dimension_semantics=semantics, vmem_limit_bytes=V7X_VMEM_LIMIT_BYTES)


def _in_proj_kernel(x_ref, gs_ref, sh_ref, w_ref, z_ref, u_ref):
    @pl.when(pl.program_id(1) == 0)
    def _():
        x = x_ref[...]
        inv = lax.rsqrt(jnp.mean(x * x, axis=-1, keepdims=True) + NORM_EPS)
        u_ref[...] = (x * inv * gs_ref[0] + sh_ref[0]).astype(jnp.bfloat16)

    z_ref[...] = jnp.dot(u_ref[...], w_ref[...], preferred_element_type=jnp.float32)


def in_proj(x2d, gs, sh, w_bf16, seq_len, tm=IN_ROW_TILE, tn=IN_TILE_N):
    M, D = x2d.shape
    Np = w_bf16.shape[1]
    bm = gs.shape[0]
    mod_idx = (lambda i, j: (i * tm // seq_len, 0, 0)) if bm > 1 else (lambda i, j: (0, 0, 0))
    return pl.pallas_call(
        _in_proj_kernel,
        grid=(M // tm, Np // tn),
        in_specs=[
            pl.BlockSpec((tm, D), lambda i, j: (i, 0)),
            pl.BlockSpec((1, 1, D), mod_idx),
            pl.BlockSpec((1, 1, D), mod_idx),
            pl.BlockSpec((D, tn), lambda i, j: (0, j)),
        ],
        out_specs=pl.BlockSpec((tm, tn), lambda i, j: (i, j)),
        out_shape=jax.ShapeDtypeStruct((M, Np), jnp.float32),
        scratch_shapes=[pltpu.VMEM((tm, D), jnp.bfloat16)],
        compiler_params=_compiler_params(("parallel", "arbitrary")),
        name="in_proj",
    )(x2d, gs, sh, w_bf16)


def _out_proj_kernel(conv_ref, hx1_ref, base_ref, rg_ref, hg_ref, att_ref, w_ref, x_ref, g1_ref, gs2_ref, sh2_ref,
                     x1_ref, u2_ref):
    bf16 = jnp.bfloat16
    y_hy = hx1_ref[...] * conv_ref[...] + base_ref[...]
    m = jnp.dot(y_hy.astype(bf16), w_ref[0:GROUP_W, :], preferred_element_type=jnp.float32)
    for k, ref in enumerate((rg_ref, hg_ref, att_ref), start=1):
        m += jnp.dot(ref[...].astype(bf16), w_ref[k * GROUP_W:(k + 1) * GROUP_W, :], preferred_element_type=jnp.float32)
    x1 = x_ref[...] + g1_ref[0] * m
    x1_ref[...] = x1
    inv = lax.rsqrt(jnp.mean(x1 * x1, axis=-1, keepdims=True) + NORM_EPS)
    u2_ref[...] = (x1 * inv * gs2_ref[0] + sh2_ref[0]).astype(jnp.bfloat16)


def out_proj(mix_parts, w_bf16, x2d, g1, gs2, sh2, seq_len, tm=ROW_TILE):
    M, D = x2d.shape
    K = w_bf16.shape[0]
    bm = g1.shape[0]
    mod_idx = (lambda i: (i * tm // seq_len, 0, 0)) if bm > 1 else (lambda i: (0, 0, 0))
    mod_spec = pl.BlockSpec((1, 1, D), mod_idx)
    part_spec = pl.BlockSpec((tm, GROUP_W), lambda i: (i, 0))
    return pl.pallas_call(
        _out_proj_kernel,
        grid=(M // tm,),
        in_specs=[
            part_spec, part_spec, part_spec, part_spec, part_spec, part_spec,
            pl.BlockSpec((K, D), lambda i: (0, 0)),
            pl.BlockSpec((tm, D), lambda i: (i, 0)),
            mod_spec, mod_spec, mod_spec,
        ],
        out_specs=[pl.BlockSpec((tm, D), lambda i: (i, 0)), pl.BlockSpec((tm, D), lambda i: (i, 0))],
        out_shape=[jax.ShapeDtypeStruct((M, D), jnp.float32), jax.ShapeDtypeStruct((M, D), jnp.bfloat16)],
        compiler_params=_compiler_params(("parallel",)),
        name="out_proj",
    )(*mix_parts, w_bf16, x2d, g1, gs2, sh2)


def _conv_ffn_kernel(u_ref, up_ref, un_ref, wv_ref, wg_ref, cwv_ref, cwg_ref, cbv_ref, cbg_ref,
                     dn_ref, x1_ref, g2_ref, o_ref, ux_s, *, tm, seq_len):
    i = pl.program_id(0)
    j = pl.program_id(1)

    @pl.when(j == 0)
    def _():
        ux_s[0:HALO_ROWS, :] = up_ref[...]
        ux_s[HALO_ROWS:HALO_ROWS + tm, :] = u_ref[...]
        ux_s[HALO_ROWS + tm:, :] = un_ref[...]

    ux = ux_s[...]
    row = lax.broadcasted_iota(jnp.int32, (tm, 1), 0)
    pos = (row + i * tm) % seq_len
    interior_ends = seq_len < tm
    first_has_prev = ((i * tm) % seq_len) != 0
    last_has_next = ((i * tm + tm - 1) % seq_len) != seq_len - 1

    def branch(w_ref, cw_ref, cb_ref, cols):
        hx = jnp.dot(ux, w_ref[:, cols], preferred_element_type=jnp.float32)
        h = hx[HALO_ROWS:HALO_ROWS + tm]
        h_before = jnp.where(first_has_prev, hx[HALO_ROWS - 1:HALO_ROWS], 0.0)
        h_after = jnp.where(last_has_next, hx[HALO_ROWS + tm:HALO_ROWS + tm + 1], 0.0)
        h_m1 = jnp.where(row == 0, h_before, pltpu.roll(h, 1, axis=0))
        h_p1 = jnp.where(row == tm - 1, h_after, pltpu.roll(h, tm - 1, axis=0))
        if interior_ends:
            h_m1 = jnp.where(pos != 0, h_m1, 0.0)
            h_p1 = jnp.where(pos != seq_len - 1, h_p1, 0.0)
        cw = cw_ref[:, cols]
        return cw[0:1] * h_m1 + cw[1:2] * h + cw[2:3] * h_p1 + cb_ref[:, cols]

    tf = wv_ref.shape[1]
    acts = []
    for s0 in range(0, tf, FFN_SUB_F):
        cols = slice(s0, s0 + FFN_SUB_F)
        val = branch(wv_ref, cwv_ref, cbv_ref, cols)
        gate = branch(wg_ref, cwg_ref, cbg_ref, cols)
        acts.append((gate * jax.nn.sigmoid(gate) * val).astype(jnp.bfloat16))
    part = jnp.dot(jnp.concatenate(acts, axis=1), dn_ref[...], preferred_element_type=jnp.float32)

    @pl.when(j == 0)
    def _():
        o_ref[...] = part

    @pl.when(j > 0)
    def _():
        o_ref[...] += part

    @pl.when(j == pl.num_programs(1) - 1)
    def _():
        o_ref[...] = x1_ref[...] + g2_ref[0] * o_ref[...]


def conv_ffn(u2, up_bf16, conv_w, conv_b, down_bf16, x1, g2, seq_len, tm=ROW_TILE, tf=FFN_TILE_F):
    M, D = x1.shape
    F = down_bf16.shape[0]
    nf = F // tf
    bm = g2.shape[0]
    hb = tm // HALO_ROWS
    n_halo = M // HALO_ROWS
    mod_idx = (lambda i, j: (i * tm // seq_len, 0, 0)) if bm > 1 else (lambda i, j: (0, 0, 0))
    conv_b2 = conv_b.reshape(1, 2 * F)
    kern = functools.partial(_conv_ffn_kernel, tm=tm, seq_len=seq_len)
    return pl.pallas_call(
        kern,
        grid=(M // tm, nf),
        in_specs=[
            pl.BlockSpec((tm, D), lambda i, j: (i, 0)),
            pl.BlockSpec((HALO_ROWS, D), lambda i, j: (jnp.maximum(i * hb - 1, 0), 0)),
            pl.BlockSpec((HALO_ROWS, D), lambda i, j: (jnp.minimum((i + 1) * hb, n_halo - 1), 0)),
            pl.BlockSpec((D, tf), lambda i, j: (0, j)),
            pl.BlockSpec((D, tf), lambda i, j: (0, j + nf)),
            pl.BlockSpec((FFN_CONV, tf), lambda i, j: (0, j)),
            pl.BlockSpec((FFN_CONV, tf), lambda i, j: (0, j + nf)),
            pl.BlockSpec((1, tf), lambda i, j: (0, j)),
            pl.BlockSpec((1, tf), lambda i, j: (0, j + nf)),
            pl.BlockSpec((tf, D), lambda i, j: (j, 0)),
            pl.BlockSpec((tm, D), lambda i, j: (i, 0)),
            pl.BlockSpec((1, 1, D), mod_idx),
        ],
        out_specs=pl.BlockSpec((tm, D), lambda i, j: (i, 0)),
        out_shape=jax.ShapeDtypeStruct((M, D), jnp.float32),
        scratch_shapes=[pltpu.VMEM((tm + 2 * HALO_ROWS, D), jnp.bfloat16)],
        compiler_params=_compiler_params(("parallel", "arbitrary")),
        name="conv_ffn",
    )(u2, u2, u2, up_bf16, up_bf16, conv_w, conv_w, conv_b2, conv_b2, down_bf16, x1, g2)


def _chunk_recurrence(a, b, h, reverse):
    n = a.shape[0]
    row = lax.broadcasted_iota(jnp.int32, (n, 1), 0)
    s = 1
    while s < n:
        shift = n - s if reverse else s
        valid = (row < n - s) if reverse else (row >= s)
        a_sh = jnp.where(valid, pltpu.roll(a, shift, axis=0), 1.0)
        b_sh = jnp.where(valid, pltpu.roll(b, shift, axis=0), 0.0)
        b = b + a * b_sh
        a = a * a_sh
        s *= 2
    hh = b + a * h
    return hh, (hh[0:1] if reverse else hh[n - 1:n])


def _rglru_kernel(x_ref, g_ref, cw_ref, cb_ref, w_ref, gb_ref, ca_ref, h0_ref, y_ref, hT_ref, rx_s, hf_s, hb_s,
                  *, seq_len, chunk):
    n_chunks = seq_len // chunk
    row = lax.broadcasted_iota(jnp.int32, (chunk, 1), 0)
    cw = cw_ref[...]
    cb = cb_ref[...]

    def conv_body(k, carry):
        c0 = pl.multiple_of(k * chunk, chunk)
        x = x_ref[pl.ds(c0, chunk), :]
        before = x_ref[pl.ds(pl.multiple_of(jnp.maximum(c0 - 8, 0), 8), 8), :]
        after = x_ref[pl.ds(pl.multiple_of(jnp.minimum(c0 + chunk, seq_len - 8), 8), 8), :]
        pos = row + c0
        x_m1 = jnp.where(row == 0, before[7:8], pltpu.roll(x, 1, axis=0))
        x_p1 = jnp.where(row == chunk - 1, after[0:1], pltpu.roll(x, chunk - 1, axis=0))
        x_p2 = jnp.where(row == chunk - 2, after[0:1],
                         jnp.where(row == chunk - 1, after[1:2], pltpu.roll(x, chunk - 2, axis=0)))
        x_m1 = jnp.where(pos >= 1, x_m1, 0.0)
        x_p1 = jnp.where(pos < seq_len - 1, x_p1, 0.0)
        x_p2 = jnp.where(pos < seq_len - 2, x_p2, 0.0)
        rx_s[pl.ds(c0, chunk), :] = cw[0:1] * x_m1 + cw[1:2] * x + cw[2:3] * x_p1 + cw[3:4] * x_p2 + cb
        return carry

    lax.fori_loop(0, n_chunks, conv_body, 0)

    def affine(c0, d):
        rx = rx_s[pl.ds(c0, chunk), :]
        w = w_ref[0][:, d * 2 * RG_DH:(d + 1) * 2 * RG_DH]
        gates = jnp.dot(rx.astype(jnp.bfloat16), w, preferred_element_type=jnp.float32)
        gates = gates + gb_ref[0][:, d * 2 * RG_DH:(d + 1) * 2 * RG_DH]
        r = jax.nn.sigmoid(gates[:, :RG_DH])
        i = jax.nn.sigmoid(gates[:, RG_DH:])
        a = jnp.exp(ca_ref[d:d + 1, :] * r)
        b = jnp.sqrt(1.0 - a * a) * (i * rx)
        return a, b

    def scan_body(k, carry):
        h_f, h_b = carry
        cf = pl.multiple_of(k * chunk, chunk)
        cbk = pl.multiple_of((n_chunks - 1 - k) * chunk, chunk)
        a_f, b_f = affine(cf, 0)
        a_b, b_b = affine(cbk, 1)
        hh_f, h_f = _chunk_recurrence(a_f, b_f, h_f, reverse=False)
        hh_b, h_b = _chunk_recurrence(a_b, b_b, h_b, reverse=True)
        hf_s[pl.ds(cf, chunk), :] = hh_f
        hb_s[pl.ds(cbk, chunk), :] = hh_b
        return h_f, h_b

    h_f, h_b = lax.fori_loop(0, n_chunks, scan_body, (h0_ref[0][0:1, :], h0_ref[0][1:2, :]))
    hT_ref[0] = jnp.concatenate([h_f, h_b], axis=0)

    def out_body(k, carry):
        c0 = pl.multiple_of(k * chunk, chunk)
        y_ref[pl.ds(c0, chunk), :] = ((hf_s[pl.ds(c0, chunk), :] + hb_s[pl.ds(c0, chunk), :])
                                      * jax.nn.gelu(g_ref[pl.ds(c0, chunk), :]))
        return carry

    lax.fori_loop(0, n_chunks, out_body, 0)


RG_X_COL = 3 * HY_W
RG_GATE_COL = RG_X_COL + RG_W


def rglru(z2d, lp, h0, seq_len, chunk=128):
    M = z2d.shape[0]
    B = M // seq_len
    f32 = jnp.float32
    gw = lp['rg_gate_w']
    w = jnp.transpose(gw, (2, 3, 0, 1, 4)).reshape(RG_H, RG_DH, 4 * RG_DH).astype(jnp.bfloat16)
    gb = jnp.transpose(lp['rg_gate_b'].reshape(2, 2, RG_H, RG_DH), (2, 0, 1, 3)).reshape(RG_H, 1, 4 * RG_DH)
    ca = -RG_C * jax.nn.softplus(-lp['rg_lambda'].astype(f32))
    xcol = RG_X_COL // RG_DH
    gcol = RG_GATE_COL // RG_DH
    kern = functools.partial(_rglru_kernel, seq_len=seq_len, chunk=chunk)
    return pl.pallas_call(
        kern,
        grid=(B, RG_H),
        in_specs=[
            pl.BlockSpec((seq_len, RG_DH), lambda b, h: (b, xcol + h)),
            pl.BlockSpec((seq_len, RG_DH), lambda b, h: (b, gcol + h)),
            pl.BlockSpec((RG_CONV, RG_DH), lambda b, h: (0, h)),
            pl.BlockSpec((1, RG_DH), lambda b, h: (0, h)),
            pl.BlockSpec((1, RG_DH, 4 * RG_DH), lambda b, h: (h, 0, 0)),
            pl.BlockSpec((1, 1, 4 * RG_DH), lambda b, h: (h, 0, 0)),
            pl.BlockSpec((2, RG_DH), lambda b, h: (0, h)),
            pl.BlockSpec((1, 2, RG_DH), lambda b, h: (b, 0, h)),
        ],
        out_specs=[
            pl.BlockSpec((seq_len, RG_DH), lambda b, h: (b, h)),
            pl.BlockSpec((1, 2, RG_DH), lambda b, h: (b, 0, h)),
        ],
        out_shape=[jax.ShapeDtypeStruct((M, RG_W), f32), jax.ShapeDtypeStruct((B, 2, RG_W), f32)],
        scratch_shapes=[pltpu.VMEM((seq_len, RG_DH), f32), pltpu.VMEM((seq_len, RG_DH), f32),
                        pltpu.VMEM((seq_len, RG_DH), f32)],
        compiler_params=_compiler_params(("parallel", "parallel")),
        name="rglru",
    )(z2d, z2d, lp['rg_conv_w'], lp['rg_conv_b'].reshape(1, RG_W), w, gb, ca, h0.astype(f32))


HG_BLOCK = 256


def _chunk_cumsum(x, reverse):
    n = x.shape[0]
    r = lax.broadcasted_iota(jnp.int32, (n, 1), 0) % HG_CHUNK
    s = 1
    while s < HG_CHUNK:
        shift = n - s if reverse else s
        valid = (r < HG_CHUNK - s) if reverse else (r >= s)
        x = x + jnp.where(valid, pltpu.roll(x, shift, axis=0), 0.0)
        s *= 2
    return x


def _pair_levels(rows, reverse):
    r = lax.broadcasted_iota(jnp.int32, (rows, rows), 0)
    c = lax.broadcasted_iota(jnp.int32, (rows, rows), 1)
    level = jnp.zeros((rows, rows), jnp.int32)
    size = HG_CHUNK
    while size < rows:
        level = level + jnp.where((r // size) == (c // size), 0, 1)
        size *= 2
    return jnp.where((c < r) if reverse else (c > r), -1, level)


def _hgrn_block(q, v, graw, lb, s_t, level, reverse):
    bf16 = jnp.bfloat16
    rows = q.shape[0]
    nseg = rows // HG_CHUNK
    f = lb + (1.0 - lb) * jax.nn.sigmoid(graw)
    k = 1.0 - f
    bc = _chunk_cumsum(jnp.log(f), reverse)
    mid = HG_CHUNK // 2 - 1
    ref_row = HG_CHUNK - 1 - mid if reverse else mid
    tot_row = 0 if reverse else HG_CHUNK - 1
    nt = (((1,), (1,)), ((), ()))
    segs = [slice(j * HG_CHUNK, (j + 1) * HG_CHUNK) for j in range(nseg)]
    before = [None] * nseg
    acc = jnp.zeros_like(lb)
    for j in (range(nseg - 1, -1, -1) if reverse else range(nseg)):
        before[j] = acc
        acc = acc + bc[segs[j]][tot_row:tot_row + 1]
    total = acc
    full = lambda rows_: [jnp.broadcast_to(r_, (HG_CHUNK, r_.shape[1])) for r_ in rows_]
    g_blk = bc + jnp.concatenate(full(before), axis=0)
    ref = jnp.concatenate(full([bc[s][ref_row:ref_row + 1] for s in segs]), axis=0)
    att = lax.dot_general((q * jnp.exp(bc - ref)).astype(bf16), (k * jnp.exp(ref - bc)).astype(bf16), nt,
                          preferred_element_type=jnp.float32)
    att = jnp.where(level == 0, att, 0.0)
    lvl, group = 1, 2
    while group <= nseg:
        half = group // 2
        bound = [before[(j // group) * group + (half - 1 if reverse else half)] for j in range(nseg)]
        bound = jnp.concatenate(full(bound), axis=0)
        qf = (q * jnp.exp(jnp.minimum(g_blk - bound, 0.0))).astype(bf16)
        kf = (k * jnp.exp(jnp.minimum(bound - g_blk, 0.0))).astype(bf16)
        att = jnp.where(level == lvl, lax.dot_general(qf, kf, nt, preferred_element_type=jnp.float32), att)
        lvl, group = lvl + 1, group * 2
    v16 = v.astype(bf16)
    o = jnp.dot(att.astype(bf16), v16, preferred_element_type=jnp.float32)
    o = o + lax.dot_general((q * jnp.exp(g_blk)).astype(bf16), s_t.astype(bf16), nt,
                            preferred_element_type=jnp.float32)
    kv_t = lax.dot_general(v16, (k * jnp.exp(total - g_blk)).astype(bf16), (((0,), (0,)), ((), ())),
                           preferred_element_type=jnp.float32)
    return o, jnp.exp(total) * s_t + kv_t


def _hgrn_kernel(q_ref, v_ref, ff_ref, fb_ref, gate_ref, lb_ref, ng_ref, s0_ref, y_ref, sT_ref, of_s, ob_s,
                 *, seq_len, block):
    n_blocks = seq_len // block
    mask_f = _pair_levels(block, reverse=False)
    mask_b = _pair_levels(block, reverse=True)

    def scan_body(i, carry):
        s_f, s_b = carry
        cf = pl.multiple_of(i * block, block)
        cb = pl.multiple_of((n_blocks - 1 - i) * block, block)
        o_f, s_f = _hgrn_block(q_ref[pl.ds(cf, block), :], v_ref[pl.ds(cf, block), :], ff_ref[pl.ds(cf, block), :],
                               lb_ref[0:1, :], s_f, mask_f, reverse=False)
        o_b, s_b = _hgrn_block(q_ref[pl.ds(cb, block), :], v_ref[pl.ds(cb, block), :], fb_ref[pl.ds(cb, block), :],
                               lb_ref[1:2, :], s_b, mask_b, reverse=True)
        of_s[pl.ds(cf, block), :] = o_f
        ob_s[pl.ds(cb, block), :] = o_b
        return s_f, s_b

    s_f, s_b = lax.fori_loop(0, n_blocks, scan_body, (s0_ref[0, 0, 0].T, s0_ref[0, 1, 0].T))
    sT_ref[0, 0, 0] = s_f.T
    sT_ref[0, 1, 0] = s_b.T

    def out_body(i, carry):
        c0 = pl.multiple_of(i * block, block)
        o = of_s[pl.ds(c0, block), :] + ob_s[pl.ds(c0, block), :]
        inv = lax.rsqrt(jnp.mean(o * o, axis=-1, keepdims=True) + NORM_EPS)
        gate = gate_ref[pl.ds(c0, block), :]
        y_ref[pl.ds(c0, block), :] = o * inv * ng_ref[...] * (gate * jax.nn.sigmoid(gate))
        return carry

    lax.fori_loop(0, n_blocks, out_body, 0)


HG_Q_COL = RG_GATE_COL + RG_W


def hgrn2(z2d, lp, lb, s0, seq_len):
    M = z2d.shape[0]
    B = M // seq_len
    f32 = jnp.float32
    block = min(HG_BLOCK, seq_len)
    col = HG_Q_COL // HG_DK
    kern = functools.partial(_hgrn_kernel, seq_len=seq_len, block=block)

    def zspec(part):
        return pl.BlockSpec((seq_len, HG_DK), lambda b, h: (b, col + part * HG_H + h))

    return pl.pallas_call(
        kern,
        grid=(B, HG_H),
        in_specs=[
            zspec(0), zspec(1), zspec(2), zspec(3), zspec(4),
            pl.BlockSpec((2, HG_DK), lambda b, h: (0, h)),
            pl.BlockSpec((1, HG_DV), lambda b, h: (0, h)),
            pl.BlockSpec((1, 2, 1, HG_DK, HG_DV), lambda b, h: (b, 0, h, 0, 0)),
        ],
        out_specs=[
            pl.BlockSpec((seq_len, HG_DV), lambda b, h: (b, h)),
            pl.BlockSpec((1, 2, 1, HG_DK, HG_DV), lambda b, h: (b, 0, h, 0, 0)),
        ],
        out_shape=[jax.ShapeDtypeStruct((M, HG_W), f32), jax.ShapeDtypeStruct((B, 2, HG_H, HG_DK, HG_DV), f32)],
        scratch_shapes=[pltpu.VMEM((seq_len, HG_DV), f32), pltpu.VMEM((seq_len, HG_DV), f32)],
        compiler_params=_compiler_params(("parallel", "parallel")),
        name="hgrn2",
    )(z2d, z2d, z2d, z2d, z2d, lb.astype(f32), lp['hg_norm_g'].reshape(1, HG_W), s0.astype(f32))


ATT_Q_COL = HG_Q_COL + 5 * HG_W
ATT_KV_COL = ATT_Q_COL + ATT_W
LANES = 128
ATT_TQ = 512


def _head_mean_square(x):
    ss = x * x
    hi = ss.astype(jnp.bfloat16)
    lo = (ss - hi.astype(jnp.float32)).astype(jnp.bfloat16)
    r = lax.broadcasted_iota(jnp.int32, (LANES, LANES), 0) // HEAD_DIM
    c = lax.broadcasted_iota(jnp.int32, (LANES, LANES), 1) // HEAD_DIM
    ones = jnp.where(r == c, 1.0, 0.0).astype(jnp.bfloat16)
    tot = jnp.dot(hi, ones, preferred_element_type=jnp.float32) + jnp.dot(lo, ones, preferred_element_type=jnp.float32)
    return tot * (1.0 / HEAD_DIM)


def _qk_norm_rope(x, gain, cos, sin):
    y = x * lax.rsqrt(_head_mean_square(x) + NORM_EPS) * gain
    if cos is None:
        return y
    lane = lax.broadcasted_iota(jnp.int32, (1, LANES), 1) % (HEAD_DIM // 2)
    quarter = HEAD_DIM // 4
    partner = jnp.where(lane < quarter, pltpu.roll(y, LANES - quarter, axis=1), pltpu.roll(y, quarter, axis=1))
    return y * cos + partner * sin


def _both_halves(x, g):
    lane = lax.broadcasted_iota(jnp.int32, (1, LANES), 1)
    keep = (lane < HEAD_DIM) if g == 0 else (lane >= HEAD_DIM)
    return jnp.where(keep, x, pltpu.roll(x, HEAD_DIM, axis=1))


def _attn_kernel(*refs, latent, tq, seq_len):
    bf16 = jnp.bfloat16
    if latent:
        (q_ref, kvm_ref, kvp_ref, kvn_ref, cosm_ref, sinm_ref, cosp_ref, sinp_ref, cosn_ref, sinn_ref,
         ck_ref, cv_ref, qg_ref, kg_ref, sink_ref, y_ref, q_s, k2_s, v2_s, ck2_s, cv2_s) = refs
    else:
        (q_ref, kvm_ref, qg_ref, kg_ref, sink_ref, y_ref, ka_ref, q_s, k2_s, v2_s) = refs
    i = pl.program_id(1)
    blk = ATT_BLOCK

    if latent:
        pieces = [(kvp_ref, cosp_ref, sinp_ref, 0, blk), (kvm_ref, cosm_ref, sinm_ref, blk, tq),
                  (kvn_ref, cosn_ref, sinn_ref, blk + tq, blk)]
    else:
        pieces = [(kvm_ref, None, None, 0, tq)]
    for kv_ref, cos_ref, sin_ref, r0, n in pieces:
        kv = kv_ref[...]
        kn = _qk_norm_rope(kv[:, :LANES], kg_ref[...], None if cos_ref is None else cos_ref[...],
                           None if sin_ref is None else sin_ref[...])
        if not latent:
            ka_ref[...] = kn
        for g in range(ATT_KV):
            k2_s[g, r0:r0 + n, :] = _both_halves(kn, g).astype(bf16)
            v2_s[g, r0:r0 + n, :] = _both_halves(kv[:, LANES:], g).astype(bf16)
    if latent:
        for g in range(ATT_KV):
            ck2_s[g] = _both_halves(ck_ref[0], g).astype(bf16)
            cv2_s[g] = _both_halves(cv_ref[0], g).astype(bf16)

    for p in range(ATT_W // LANES):
        qn = _qk_norm_rope(q_ref[:, p * LANES:(p + 1) * LANES], qg_ref[...],
                           cosm_ref[...] if latent else None, sinm_ref[...] if latent else None)
        q_s[:, p * LANES:(p + 1) * LANES] = (qn * ATT_SCALE).astype(bf16)

    rows = ATT_G * blk
    lane = lax.broadcasted_iota(jnp.int32, (1, LANES), 1)
    low = lane < HEAD_DIM
    head_of_row = lax.broadcasted_iota(jnp.int32, (rows, 1), 0) // blk
    nt = (((1,), (1,)), ((), ()))
    for j in range(tq // blk):
        for g in range(ATT_KV):
            stack = []
            for p in (2 * g, 2 * g + 1):
                qb = q_s[j * blk:(j + 1) * blk, p * LANES:(p + 1) * LANES]
                stack.append(jnp.where(low, qb, jnp.zeros_like(qb)))
                stack.append(jnp.where(low, jnp.zeros_like(qb), qb))
            qg = jnp.concatenate(stack, axis=0)
            sink = jnp.zeros((rows, 1), jnp.float32)
            for hh in range(ATT_G):
                sink = jnp.where(head_of_row == hh, sink_ref[ATT_G * g + hh:ATT_G * g + hh + 1, 0:1], sink)
            if latent:
                nk = 3 * blk
                s_w = lax.dot_general(qg, k2_s[g, j * blk:j * blk + nk, :], nt, preferred_element_type=jnp.float32)
                a = lax.broadcasted_iota(jnp.int32, (rows, nk), 0) % blk
                c = lax.broadcasted_iota(jnp.int32, (rows, nk), 1)
                kpos = c + (i * tq + j * blk - blk)
                ok = (jnp.abs(c - blk - a) <= WINDOW) & (kpos >= 0) & (kpos < seq_len)
                s_w = jnp.where(ok, s_w, NEG_INF)
                s_c = lax.dot_general(qg, ck2_s[g], nt, preferred_element_type=jnp.float32)
                m = jnp.maximum(jnp.maximum(jnp.max(s_w, axis=-1, keepdims=True),
                                            jnp.max(s_c, axis=-1, keepdims=True)), sink)
                e_w = jnp.exp(s_w - m)
                e_c = jnp.exp(s_c - m)
                den = jnp.sum(e_w, axis=-1, keepdims=True) + jnp.sum(e_c, axis=-1, keepdims=True) + jnp.exp(sink - m)
                o = (jnp.dot(e_w.astype(bf16), v2_s[g, j * blk:j * blk + nk, :], preferred_element_type=jnp.float32)
                     + jnp.dot(e_c.astype(bf16), cv2_s[g], preferred_element_type=jnp.float32))
            else:
                s = lax.dot_general(qg, k2_s[g], nt, preferred_element_type=jnp.float32)
                m = jnp.maximum(jnp.max(s, axis=-1, keepdims=True), sink)
                e = jnp.exp(s - m)
                den = jnp.sum(e, axis=-1, keepdims=True) + jnp.exp(sink - m)
                o = jnp.dot(e.astype(bf16), v2_s[g], preferred_element_type=jnp.float32)
            o = o / den
            for t, p in enumerate((2 * g, 2 * g + 1)):
                y_ref[j * blk:(j + 1) * blk, p * LANES:(p + 1) * LANES] = jnp.where(
                    low, o[2 * t * blk:(2 * t + 1) * blk], o[(2 * t + 1) * blk:(2 * t + 2) * blk])


def _rope_tables(seq_len):
    f32 = jnp.float32
    half = HEAD_DIM // 2
    t = jnp.arange(seq_len)
    row = (t // GRID_W).astype(f32)
    col = (t % GRID_W).astype(f32)
    inv = ROPE_BASE ** (-jnp.arange(0, half, 2, dtype=f32) / half)

    def tab(pos):
        ang = pos[:, None] * inv[None, :]
        return (jnp.concatenate([jnp.cos(ang), jnp.cos(ang)], axis=-1),
                jnp.concatenate([-jnp.sin(ang), jnp.sin(ang)], axis=-1))

    cr, sr = tab(row)
    cc, sc = tab(col)
    cos = jnp.concatenate([cr, cc], axis=-1)
    sin = jnp.concatenate([sr, sc], axis=-1)
    return jnp.tile(cos, (1, 2)), jnp.tile(sin, (1, 2))


def attention(z2d, lp, seq_len, ctx_k=None, ctx_v=None):
    M = z2d.shape[0]
    B = M // seq_len
    f32 = jnp.float32
    bf16 = jnp.bfloat16
    latent = ctx_k is not None
    tq = min(ATT_TQ, seq_len)
    nq = seq_len // tq
    qg = jnp.tile(lp['att_qn_g'].astype(f32), 2).reshape(1, LANES)
    kg = jnp.tile(lp['att_kn_g'].astype(f32), 2).reshape(1, LANES)
    sink = jnp.broadcast_to(lp['att_sink'].astype(f32)[:, None], (ATT_H, LANES))
    qcol = ATT_Q_COL // ATT_W
    kvcol = ATT_KV_COL // (2 * LANES)
    kern = functools.partial(_attn_kernel, latent=latent, tq=tq, seq_len=seq_len)
    q_spec = pl.BlockSpec((tq, ATT_W), lambda b, i: (b * nq + i, qcol))
    kv_spec = pl.BlockSpec((tq, 2 * LANES), lambda b, i: (b * nq + i, kvcol))
    small = lambda shape: pl.BlockSpec(shape, lambda b, i: (0, 0))
    y_spec = pl.BlockSpec((tq, ATT_W), lambda b, i: (b * nq + i, 0))
    params = _compiler_params(("parallel", "parallel"))
    if not latent:
        return pl.pallas_call(
            kern, grid=(B, nq),
            in_specs=[q_spec, kv_spec, small((1, LANES)), small((1, LANES)), small((ATT_H, LANES))],
            out_specs=[y_spec, pl.BlockSpec((tq, LANES), lambda b, i: (b * nq + i, 0))],
            out_shape=[jax.ShapeDtypeStruct((M, ATT_W), f32), jax.ShapeDtypeStruct((M, LANES), f32)],
            scratch_shapes=[pltpu.VMEM((tq, ATT_W), bf16), pltpu.VMEM((ATT_KV, tq, LANES), bf16),
                            pltpu.VMEM((ATT_KV, tq, LANES), bf16)],
            compiler_params=params, name="attn_context",
        )(z2d, z2d, qg, kg, sink)
    P = ctx_k.shape[1]
    hb = tq // ATT_BLOCK
    nblk = seq_len // ATT_BLOCK
    cos, sin = _rope_tables(seq_len)
    prev_idx = lambda b, i: (b * nblk + jnp.maximum(i * hb - 1, 0), kvcol)
    next_idx = lambda b, i: (b * nblk + jnp.minimum((i + 1) * hb, nblk - 1), kvcol)
    tab_m = pl.BlockSpec((tq, LANES), lambda b, i: (i, 0))
    tab_p = pl.BlockSpec((ATT_BLOCK, LANES), lambda b, i: (jnp.maximum(i * hb - 1, 0), 0))
    tab_n = pl.BlockSpec((ATT_BLOCK, LANES), lambda b, i: (jnp.minimum((i + 1) * hb, nblk - 1), 0))
    ctx_spec = pl.BlockSpec((1, P, LANES), lambda b, i: (b, 0, 0))
    nk_all = tq + 2 * ATT_BLOCK
    return pl.pallas_call(
        kern, grid=(B, nq),
        in_specs=[q_spec, kv_spec, pl.BlockSpec((ATT_BLOCK, 2 * LANES), prev_idx),
                  pl.BlockSpec((ATT_BLOCK, 2 * LANES), next_idx),
                  tab_m, tab_m, tab_p, tab_p, tab_n, tab_n, ctx_spec, ctx_spec,
                  small((1, LANES)), small((1, LANES)), small((ATT_H, LANES))],
        out_specs=y_spec,
        out_shape=jax.ShapeDtypeStruct((M, ATT_W), f32),
        scratch_shapes=[pltpu.VMEM((tq, ATT_W), bf16), pltpu.VMEM((ATT_KV, nk_all, LANES), bf16),
                        pltpu.VMEM((ATT_KV, nk_all, LANES), bf16), pltpu.VMEM((ATT_KV, P, LANES), bf16),
                        pltpu.VMEM((ATT_KV, P, LANES), bf16)],
        compiler_params=params, name="attn_latent",
    )(z2d, z2d, z2d, z2d, cos, sin, cos, sin, cos, sin, ctx_k.astype(f32), ctx_v.astype(f32), qg, kg, sink)


HY_CHUNK = 256
HY_TILE_F = 256
DFT_SPLIT = 64


def _hy_pre_kernel(hv_ref, h1_ref, h2_ref, cw_ref, cb_ref, skip_ref, w_ref, x1_ref, base_ref, *, seq_len, chunk):
    n_chunks = seq_len // chunk
    row = lax.broadcasted_iota(jnp.int32, (chunk, 1), 0)

    def conv(ref, part, c0):
        x = ref[pl.ds(c0, chunk), :]
        before = ref[pl.ds(pl.multiple_of(jnp.maximum(c0 - 8, 0), 8), 8), :]
        after = ref[pl.ds(pl.multiple_of(jnp.minimum(c0 + chunk, seq_len - 8), 8), 8), :]
        pos = row + c0
        x_m1 = jnp.where(row == 0, before[7:8], pltpu.roll(x, 1, axis=0))
        x_p1 = jnp.where(row == chunk - 1, after[0:1], pltpu.roll(x, chunk - 1, axis=0))
        x_m1 = jnp.where(pos >= 1, x_m1, 0.0)
        x_p1 = jnp.where(pos < seq_len - 1, x_p1, 0.0)
        cw = cw_ref[part]
        return cw[0:1] * x_m1 + cw[1:2] * x + cw[2:3] * x_p1 + cb_ref[part]

    def body(k, carry):
        c0 = pl.multiple_of(k * chunk, chunk)
        hv = conv(hv_ref, 0, c0)
        hx1 = conv(h1_ref, 1, c0)
        hx2 = conv(h2_ref, 2, c0)
        w = hv * hx2
        w_ref[pl.ds(c0, chunk), :] = w.astype(jnp.bfloat16)
        x1_ref[pl.ds(c0, chunk), :] = hx1
        base_ref[pl.ds(c0, chunk), :] = hx1 * (w * skip_ref[...])
        return carry

    lax.fori_loop(0, n_chunks, body, 0)


def hyena_pre(z2d, lp, seq_len):
    M = z2d.shape[0]
    B = M // seq_len
    f32 = jnp.float32
    nblk = HY_W // LANES
    chunk = min(HY_CHUNK, seq_len)
    cw = lp['hy_conv_w'].reshape(HY_SHORT, 3, HY_W).transpose(1, 0, 2)
    cb = lp['hy_conv_b'].reshape(3, 1, HY_W)
    kern = functools.partial(_hy_pre_kernel, seq_len=seq_len, chunk=chunk)
    zspec = lambda part: pl.BlockSpec((seq_len, LANES), lambda b, c: (b, part * nblk + c))
    ospec = pl.BlockSpec((seq_len, LANES), lambda b, c: (b, c))
    return pl.pallas_call(
        kern, grid=(B, nblk),
        in_specs=[zspec(0), zspec(1), zspec(2),
                  pl.BlockSpec((3, HY_SHORT, LANES), lambda b, c: (0, 0, c)),
                  pl.BlockSpec((3, 1, LANES), lambda b, c: (0, 0, c)),
                  pl.BlockSpec((1, LANES), lambda b, c: (0, c))],
        out_specs=[ospec, ospec, ospec],
        out_shape=[jax.ShapeDtypeStruct((M, HY_W), jnp.bfloat16), jax.ShapeDtypeStruct((M, HY_W), f32),
                   jax.ShapeDtypeStruct((M, HY_W), f32)],
        compiler_params=_compiler_params(("parallel", "parallel")),
        name="hyena_pre",
    )(z2d, z2d, z2d, cw, cb, lp['hy_skip'].reshape(1, HY_W).astype(f32))


def _hy_conv_kernel(g_ref, gi_ref, w_ref, kra_ref, krb_ref, ki_ref, o_ref, *, tf, seq_len):
    j = pl.program_id(1)
    x = jnp.dot(g_ref[0], w_ref[...], preferred_element_type=jnp.float32)
    xr, xs = x[:tf], x[tf:]
    ki = ki_ref[...]
    a = xr * kra_ref[...] + xs * ki
    b = xs * krb_ref[...] - xr * ki
    ab = (jnp.concatenate([a, b], axis=0) * (1.0 / seq_len)).astype(jnp.bfloat16)
    part = jnp.dot(gi_ref[0], ab, preferred_element_type=jnp.float32)

    @pl.when(j == 0)
    def _():
        o_ref[...] = part

    @pl.when(j > 0)
    def _():
        o_ref[...] += part


def _dft_tables(seq_len, tf):
    n = 2 * seq_len
    nf = seq_len // tf
    f = jnp.arange(seq_len, dtype=jnp.int32)

    def small(freqs):
        ang = ((freqs[:, None] * f[None, :]) % n).astype(jnp.float32) * (2.0 * math.pi / n)
        return jnp.cos(ang), jnp.sin(ang)

    c1, s1 = small(DFT_SPLIT * jnp.arange(seq_len // DFT_SPLIT, dtype=jnp.int32))
    c0, s0 = small(jnp.arange(DFT_SPLIT, dtype=jnp.int32))
    c = (c1[:, None, :] * c0[None, :, :] - s1[:, None, :] * s0[None, :, :]).reshape(seq_len, seq_len)
    s = (s1[:, None, :] * c0[None, :, :] + c1[:, None, :] * s0[None, :, :]).reshape(seq_len, seq_len)
    alt = (1 - 2 * (f % 2)).astype(jnp.float32)
    s = jnp.where(f[:, None] == 0, alt[None, :], s)
    fwd = jnp.concatenate([c.reshape(nf, tf, seq_len), s.reshape(nf, tf, seq_len)], axis=1)
    inv = jnp.concatenate([c.reshape(seq_len, nf, tf), s.T.reshape(seq_len, nf, tf)], axis=2).transpose(1, 0, 2)
    return fwd.astype(jnp.bfloat16), inv.astype(jnp.bfloat16)


def hyena_long_conv(w_bf16, k_long, seq_len):
    M, W = w_bf16.shape
    B = M // seq_len
    f32 = jnp.float32
    tf = min(HY_TILE_F, seq_len)
    nf = seq_len // tf
    fwd, inv = _dft_tables(seq_len, tf)
    kf = jnp.fft.rfft(k_long, axis=0)
    kr = jnp.real(kf[:seq_len]).astype(f32)
    ki = jnp.imag(kf[:seq_len]).astype(f32).at[0].set(0.0)
    kra = kr.at[0].multiply(0.5)
    krb = kr.at[0].set(0.5 * jnp.real(kf[seq_len]).astype(f32))
    kern = functools.partial(_hy_conv_kernel, tf=tf, seq_len=seq_len)
    kspec = pl.BlockSpec((tf, W), lambda b, j: (j, 0))
    return pl.pallas_call(
        kern, grid=(B, nf),
        in_specs=[pl.BlockSpec((1, 2 * tf, seq_len), lambda b, j: (j, 0, 0)),
                  pl.BlockSpec((1, seq_len, 2 * tf), lambda b, j: (j, 0, 0)),
                  pl.BlockSpec((seq_len, W), lambda b, j: (b, 0)),
                  kspec, kspec, kspec],
        out_specs=pl.BlockSpec((seq_len, W), lambda b, j: (b, 0)),
        out_shape=jax.ShapeDtypeStruct((M, W), f32),
        compiler_params=_compiler_params(("parallel", "arbitrary")),
        name="hyena_long_conv",
    )(fwd, inv, w_bf16, kra, krb, ki)


def hyena_kernel(L, w1, b1, freq, w2, b2, w3, b3, decay):
    f32 = jnp.float32
    t_idx = jnp.arange(L, dtype=f32)
    t = t_idx / (L - 1)
    ang = (2.0 * math.pi * t_idx / L)[:, None] * jnp.linspace(1e-4, HY_BANDS - 1, HY_BANDS, dtype=f32)[None, :]
    z = jnp.concatenate([t[:, None], jnp.cos(ang), -jnp.sin(ang)], axis=-1)
    h = jnp.sin(freq[0].astype(f32) * (z @ w1.astype(f32) + b1.astype(f32)))
    h = jnp.sin(freq[1].astype(f32) * (h @ w2.astype(f32) + b2.astype(f32)))
    h = (h @ w3.astype(f32) + b3.astype(f32)) * jnp.exp(-t[:, None] * jnp.abs(decay.astype(f32)))
    h_fwd, h_bwd = h[:, :HY_W], h[:, HY_W:]
    k = jnp.concatenate([h_fwd, jnp.zeros((1, HY_W), f32), h_bwd[:0:-1]], axis=0)
    return k / jnp.sum(jnp.abs(k), axis=0, keepdims=True)


def token_mixers(z, lp, lb, ctx):
    f32 = jnp.float32
    B, T, _ = z.shape
    z2d = z.reshape(B * T, -1)

    k_long = hyena_kernel(T, lp['hy_f_w1'], lp['hy_f_b1'], lp['hy_f_freq'], lp['hy_f_w2'], lp['hy_f_b2'],
                          lp['hy_f_w3'], lp['hy_f_b3'], lp['hy_decay'])
    hy_w, hy_x1, hy_base = hyena_pre(z2d, lp, T)
    hy_conv = hyena_long_conv(hy_w, k_long, T)

    h0 = jnp.zeros((B, 2, RG_W), f32) if ctx is None else ctx[2]
    y_rg, hT_rg = rglru(z2d, lp, h0, T)

    S0 = jnp.zeros((B, 2, HG_H, HG_DK, HG_DV), f32) if ctx is None else ctx[3]
    y_hg, S_hg = hgrn2(z2d, lp, lb, S0, T)

    if ctx is None:
        y_att, ka = attention(z2d, lp, T)
        ka = ka.reshape(B, T, ATT_KV, HEAD_DIM)
        va = z[..., ATT_KV_COL + LANES:ATT_KV_COL + 2 * LANES].reshape(B, T, ATT_KV, HEAD_DIM)
        new_ctx = (ka, va, hT_rg, S_hg)
    else:
        P = ctx[0].shape[1]
        y_att = attention(z2d, lp, T, ctx[0].reshape(B, P, ATT_KV * HEAD_DIM), ctx[1].reshape(B, P, ATT_KV * HEAD_DIM))
        new_ctx = None
    return (hy_conv, hy_x1, hy_base, y_rg, y_hg, y_att), new_ctx


def trunk_layer(x, mod, lp, lb, ctx):
    B, T, D = x.shape
    sh1, sc1, g1, sh2, sc2, g2 = [m[:, None, :] for m in jnp.split(mod, 6, axis=-1)]
    gs1 = lp['norm1_g'][None, None, :] * (1.0 + sc1)
    gs2 = lp['norm2_g'][None, None, :] * (1.0 + sc2)
    x2d = x.reshape(B * T, D)
    z = in_proj(x2d, gs1, sh1, lp['w_in_bf16'], T)
    mix_parts, new_ctx = token_mixers(z.reshape(B, T, IN_COLS_PAD), lp, lb, ctx)
    x1, u2 = out_proj(mix_parts, lp['w_out_bf16'], x2d, g1, gs2, sh2, T)
    x2 = conv_ffn(u2, lp['ffn_up_bf16'], lp['ffn_conv_w'], lp['ffn_conv_b'], lp['ffn_down_bf16'], x1, g2, T)
    return x2.reshape(B, T, D), new_ctx


def kernel(x_prompt, x_sample, cache_k, cache_v, state_rglru, state_hgrn, c, c_ctx, norm1_g, norm2_g, ada_w, ada_b, w_in, w_out, hy_conv_w, hy_conv_b, hy_f_w1, hy_f_b1, hy_f_freq, hy_f_w2, hy_f_b2, hy_f_w3, hy_f_b3, hy_decay, hy_skip, rg_conv_w, rg_conv_b, rg_gate_w, rg_gate_b, rg_lambda, hg_lb, hg_norm_g, att_qn_g, att_kn_g, att_sink, ffn_up, ffn_conv_w, ffn_conv_b, ffn_down):
    bf16 = jnp.bfloat16
    stacked = dict(norm1_g=norm1_g, norm2_g=norm2_g, ada_w=ada_w, ada_b=ada_b,
                   hy_conv_w=hy_conv_w, hy_conv_b=hy_conv_b, hy_f_w1=hy_f_w1, hy_f_b1=hy_f_b1, hy_f_freq=hy_f_freq,
                   hy_f_w2=hy_f_w2, hy_f_b2=hy_f_b2, hy_f_w3=hy_f_w3, hy_f_b3=hy_f_b3, hy_decay=hy_decay, hy_skip=hy_skip,
                   rg_conv_w=rg_conv_w, rg_conv_b=rg_conv_b, rg_gate_w=rg_gate_w, rg_gate_b=rg_gate_b, rg_lambda=rg_lambda,
                   hg_norm_g=hg_norm_g, att_qn_g=att_qn_g, att_kn_g=att_kn_g, att_sink=att_sink,
                   ffn_conv_w=ffn_conv_w, ffn_conv_b=ffn_conv_b,
                   w_in_bf16=jnp.pad(w_in.astype(bf16), ((0, 0), (0, 0), (0, IN_COLS_PAD - IN_COLS))),
                   w_out_bf16=w_out.astype(bf16), ffn_up_bf16=ffn_up.astype(bf16), ffn_down_bf16=ffn_down.astype(bf16))
    lbs = jnp.cumsum(jax.nn.softmax(hg_lb.astype(jnp.float32), axis=0), axis=0)
    lbs = lbs - lbs[0:1]
    y_prompt, y_sample = x_prompt, x_sample
    ks, vs, rgs, hgs = [], [], [], []
    for l in range(DEPTH):
        lp = {name: arr[l] for name, arr in stacked.items()}
        mod_p = (jax.nn.silu(c_ctx) @ lp['ada_w'] + lp['ada_b'])[None, :]
        y_prompt, ctx_new = trunk_layer(y_prompt, mod_p, lp, lbs[l], None)
        ks.append(ctx_new[0])
        vs.append(ctx_new[1])
        rgs.append(ctx_new[2])
        hgs.append(ctx_new[3])
        mod_s = jax.nn.silu(c) @ lp['ada_w'] + lp['ada_b']
        cached = (cache_k[:, l], cache_v[:, l], state_rglru[:, l], state_hgrn[:, l])
        y_sample, _ = trunk_layer(y_sample, mod_s, lp, lbs[l], cached)
    return (y_prompt, y_sample, jnp.stack(ks, axis=1), jnp.stack(vs, axis=1),
            jnp.stack(rgs, axis=1), jnp.stack(hgs, axis=1))
```

```python
import functools
import math

import jax
import jax.numpy as jnp
from jax import lax
from jax.experimental import pallas as pl
from jax.experimental.pallas import tpu as pltpu

D_MODEL = 2048
DEPTH = 2
GRID_W = 64
GROUP_W = D_MODEL // 4
MIX_W = 4 * GROUP_W
HY_W = GROUP_W
HY_BANDS = 16
HY_SHORT = 3
RG_W = GROUP_W
RG_H = 4
RG_DH = RG_W // RG_H
RG_CONV = 4
RG_PAD = ((RG_CONV - 1) // 2, RG_CONV // 2)
RG_C = 8.0
HG_W = GROUP_W
HG_H = 4
HG_DK = HG_W // HG_H
HG_DV = HG_W // HG_H
HG_CHUNK = 32
ATT_W = GROUP_W
HEAD_DIM = 64
ATT_H = ATT_W // HEAD_DIM
ATT_KV = 2
ATT_G = ATT_H // ATT_KV
WINDOW = 128
ATT_BLOCK = 128
ATT_SCALE = 1.0 / math.sqrt(HEAD_DIM)
ROPE_BASE = 10000.0
NEG_INF = -1e30
D_FF = 5632
FFN_CONV = 3
IN_SIZES = (3 * HY_W, RG_W, RG_W, HG_W, HG_W, HG_W, HG_W, HG_W, ATT_W, ATT_KV * HEAD_DIM, ATT_KV * HEAD_DIM)
IN_COLS = sum(IN_SIZES)
NORM_EPS = 1e-6

V7X_VMEM_LIMIT_BYTES = 56 * 1024 * 1024
HALO_ROWS = 16
IN_COLS_PAD = 6144
IN_TILE_N = 1024
FFN_TILE_F = 512
FFN_SUB_F = 256
ROW_TILE = 512
IN_ROW_TILE = 1024


def _compiler_params(semantics):
    return pltpu.CompilerParams(dimension_semantics=semantics, vmem_limit_bytes=V7X_VMEM_LIMIT_BYTES)


def _in_proj_kernel(x_ref, gs_ref, sh_ref, w_ref, z_ref, u_ref):
    @pl.when(pl.program_id(1) == 0)
    def _():
        x = x_ref[...]
        inv = lax.rsqrt(jnp.mean(x * x, axis=-1, keepdims=True) + NORM_EPS)
        u_ref[...] = (x * inv * gs_ref[0] + sh_ref[0]).astype(jnp.bfloat16)

    z_ref[...] = jnp.dot(u_ref[...], w_ref[...], preferred_element_type=jnp.float32)


def in_proj(x2d, gs, sh, w_bf16, seq_len, tm=IN_ROW_TILE, tn=IN_TILE_N):
    M, D = x2d.shape
    Np = w_bf16.shape[1]
    bm = gs.shape[0]
    mod_idx = (lambda i, j: (i * tm // seq_len, 0, 0)) if bm > 1 else (lambda i, j: (0, 0, 0))
    return pl.pallas_call(
        _in_proj_kernel,
        grid=(M // tm, Np // tn),
        in_specs=[
            pl.BlockSpec((tm, D), lambda i, j: (i, 0)),
            pl.BlockSpec((1, 1, D), mod_idx),
            pl.BlockSpec((1, 1, D), mod_idx),
            pl.BlockSpec((D, tn), lambda i, j: (0, j)),
        ],
        out_specs=pl.BlockSpec((tm, tn), lambda i, j: (i, j)),
        out_shape=jax.ShapeDtypeStruct((M, Np), jnp.float32),
        scratch_shapes=[pltpu.VMEM((tm, D), jnp.bfloat16)],
        compiler_params=_compiler_params(("parallel", "arbitrary")),
        name="in_proj",
    )(x2d, gs, sh, w_bf16)


def _out_proj_kernel(conv_ref, hx1_ref, base_ref, rg_ref, hg_ref, att_ref, w_ref, x_ref, g1_ref, gs2_ref, sh2_ref,
                     x1_ref, u2_ref):
    bf16 = jnp.bfloat16
    y_hy = hx1_ref[...] * conv_ref[...] + base_ref[...]
    m = jnp.dot(y_hy.astype(bf16), w_ref[0:GROUP_W, :], preferred_element_type=jnp.float32)
    for k, ref in enumerate((rg_ref, hg_ref, att_ref), start=1):
        m += jnp.dot(ref[...].astype(bf16), w_ref[k * GROUP_W:(k + 1) * GROUP_W, :], preferred_element_type=jnp.float32)
    x1 = x_ref[...] + g1_ref[0] * m
    x1_ref[...] = x1
    inv = lax.rsqrt(jnp.mean(x1 * x1, axis=-1, keepdims=True) + NORM_EPS)
    u2_ref[...] = (x1 * inv * gs2_ref[0] + sh2_ref[0]).astype(jnp.bfloat16)


def out_proj(mix_parts, w_bf16, x2d, g1, gs2, sh2, seq_len, tm=ROW_TILE):
    M, D = x2d.shape
    K = w_bf16.shape[0]
    bm = g1.shape[0]
    mod_idx = (lambda i: (i * tm // seq_len, 0, 0)) if bm > 1 else (lambda i: (0, 0, 0))
    mod_spec = pl.BlockSpec((1, 1, D), mod_idx)
    part_spec = pl.BlockSpec((tm, GROUP_W), lambda i: (i, 0))
    return pl.pallas_call(
        _out_proj_kernel,
        grid=(M // tm,),
        in_specs=[
            part_spec, part_spec, part_spec, part_spec, part_spec, part_spec,
            pl.BlockSpec((K, D), lambda i: (0, 0)),
            pl.BlockSpec((tm, D), lambda i: (i, 0)),
            mod_spec, mod_spec, mod_spec,
        ],
        out_specs=[pl.BlockSpec((tm, D), lambda i: (i, 0)), pl.BlockSpec((tm, D), lambda i: (i, 0))],
        out_shape=[jax.ShapeDtypeStruct((M, D), jnp.float32), jax.ShapeDtypeStruct((M, D), jnp.bfloat16)],
        compiler_params=_compiler_params(("parallel",)),
        name="out_proj",
    )(*mix_parts, w_bf16, x2d, g1, gs2, sh2)


def _conv_ffn_kernel(u_ref, up_ref, un_ref, wv_ref, wg_ref, cwv_ref, cwg_ref, cbv_ref, cbg_ref,
                     dn_ref, x1_ref, g2_ref, o_ref, ux_s, *, tm, seq_len):
    i = pl.program_id(0)
    j = pl.program_id(1)

    @pl.when(j == 0)
    def _():
        ux_s[0:HALO_ROWS, :] = up_ref[...]
        ux_s[HALO_ROWS:HALO_ROWS + tm, :] = u_ref[...]
        ux_s[HALO_ROWS + tm:, :] = un_ref[...]

    ux = ux_s[...]
    row = lax.broadcasted_iota(jnp.int32, (tm, 1), 0)
    pos = (row + i * tm) % seq_len
    interior_ends = seq_len < tm
    first_has_prev = ((i * tm) % seq_len) != 0
    last_has_next = ((i * tm + tm - 1) % seq_len) != seq_len - 1

    def branch(w_ref, cw_ref, cb_ref, cols):
        hx = jnp.dot(ux, w_ref[:, cols], preferred_element_type=jnp.float32)
        h = hx[HALO_ROWS:HALO_ROWS + tm]
        h_before = jnp.where(first_has_prev, hx[HALO_ROWS - 1:HALO_ROWS], 0.0)
        h_after = jnp.where(last_has_next, hx[HALO_ROWS + tm:HALO_ROWS + tm + 1], 0.0)
        h_m1 = jnp.where(row == 0, h_before, pltpu.roll(h, 1, axis=0))
        h_p1 = jnp.where(row == tm - 1, h_after, pltpu.roll(h, tm - 1, axis=0))
        if interior_ends:
            h_m1 = jnp.where(pos != 0, h_m1, 0.0)
            h_p1 = jnp.where(pos != seq_len - 1, h_p1, 0.0)
        cw = cw_ref[:, cols]
        return cw[0:1] * h_m1 + cw[1:2] * h + cw[2:3] * h_p1 + cb_ref[:, cols]

    tf = wv_ref.shape[1]
    acts = []
    for s0 in range(0, tf, FFN_SUB_F):
        cols = slice(s0, s0 + FFN_SUB_F)
        val = branch(wv_ref, cwv_ref, cbv_ref, cols)
        gate = branch(wg_ref, cwg_ref, cbg_ref, cols)
        acts.append((gate * jax.nn.sigmoid(gate) * val).astype(jnp.bfloat16))
    part = jnp.dot(jnp.concatenate(acts, axis=1), dn_ref[...], preferred_element_type=jnp.float32)

    @pl.when(j == 0)
    def _():
        o_ref[...] = part

    @pl.when(j > 0)
    def _():
        o_ref[...] += part

    @pl.when(j == pl.num_programs(1) - 1)
    def _():
        o_ref[...] = x1_ref[...] + g2_ref[0] * o_ref[...]


def conv_ffn(u2, up_bf16, conv_w, conv_b, down_bf16, x1, g2, seq_len, tm=ROW_TILE, tf=FFN_TILE_F):
    M, D = x1.shape
    F = down_bf16.shape[0]
    nf = F // tf
    bm = g2.shape[0]
    hb = tm // HALO_ROWS
    n_halo = M // HALO_ROWS
    mod_idx = (lambda i, j: (i * tm // seq_len, 0, 0)) if bm > 1 else (lambda i, j: (0, 0, 0))
    conv_b2 = conv_b.reshape(1, 2 * F)
    kern = functools.partial(_conv_ffn_kernel, tm=tm, seq_len=seq_len)
    return pl.pallas_call(
        kern,
        grid=(M // tm, nf),
        in_specs=[
            pl.BlockSpec((tm, D), lambda i, j: (i, 0)),
            pl.BlockSpec((HALO_ROWS, D), lambda i, j: (jnp.maximum(i * hb - 1, 0), 0)),
            pl.BlockSpec((HALO_ROWS, D), lambda i, j: (jnp.minimum((i + 1) * hb, n_halo - 1), 0)),
            pl.BlockSpec((D, tf), lambda i, j: (0, j)),
            pl.BlockSpec((D, tf), lambda i, j: (0, j + nf)),
            pl.BlockSpec((FFN_CONV, tf), lambda i, j: (0, j)),
            pl.BlockSpec((FFN_CONV, tf), lambda i, j: (0, j + nf)),
            pl.BlockSpec((1, tf), lambda i, j: (0, j)),
            pl.BlockSpec((1, tf), lambda i, j: (0, j + nf)),
            pl.BlockSpec((tf, D), lambda i, j: (j, 0)),
            pl.BlockSpec((tm, D), lambda i, j: (i, 0)),
            pl.BlockSpec((1, 1, D), mod_idx),
        ],
        out_specs=pl.BlockSpec((tm, D), lambda i, j: (i, 0)),
        out_shape=jax.ShapeDtypeStruct((M, D), jnp.float32),
        scratch_shapes=[pltpu.VMEM((tm + 2 * HALO_ROWS, D), jnp.bfloat16)],
        compiler_params=_compiler_params(("parallel", "arbitrary")),
        name="conv_ffn",
    )(u2, u2, u2, up_bf16, up_bf16, conv_w, conv_w, conv_b2, conv_b2, down_bf16, x1, g2)


def _chunk_recurrence(a, b, h, reverse):
    n = a.shape[0]
    row = lax.broadcasted_iota(jnp.int32, (n, 1), 0)
    s = 1
    while s < n:
        shift = n - s if reverse else s
        valid = (row < n - s) if reverse else (row >= s)
        a_sh = jnp.where(valid, pltpu.roll(a, shift, axis=0), 1.0)
        b_sh = jnp.where(valid, pltpu.roll(b, shift, axis=0), 0.0)
        b = b + a * b_sh
        a = a * a_sh
        s *= 2
    hh = b + a * h
    return hh, (hh[0:1] if reverse else hh[n - 1:n])


def _rglru_kernel(x_ref, g_ref, cw_ref, cb_ref, w_ref, gb_ref, ca_ref, h0_ref, y_ref, hT_ref, rx_s, hf_s, hb_s,
                  *, seq_len, chunk):
    n_chunks = seq_len // chunk
    row = lax.broadcasted_iota(jnp.int32, (chunk, 1), 0)
    cw = cw_ref[...]
    cb = cb_ref[...]

    def conv_body(k, carry):
        c0 = pl.multiple_of(k * chunk, chunk)
        x = x_ref[pl.ds(c0, chunk), :]
        before = x_ref[pl.ds(pl.multiple_of(jnp.maximum(c0 - 8, 0), 8), 8), :]
        after = x_ref[pl.ds(pl.multiple_of(jnp.minimum(c0 + chunk, seq_len - 8), 8), 8), :]
        pos = row + c0
        x_m1 = jnp.where(row == 0, before[7:8], pltpu.roll(x, 1, axis=0))
        x_p1 = jnp.where(row == chunk - 1, after[0:1], pltpu.roll(x, chunk - 1, axis=0))
        x_p2 = jnp.where(row == chunk - 2, after[0:1],
                         jnp.where(row == chunk - 1, after[1:2], pltpu.roll(x, chunk - 2, axis=0)))
        x_m1 = jnp.where(pos >= 1, x_m1, 0.0)
        x_p1 = jnp.where(pos < seq_len - 1, x_p1, 0.0)
        x_p2 = jnp.where(pos < seq_len - 2, x_p2, 0.0)
        rx_s[pl.ds(c0, chunk), :] = cw[0:1] * x_m1 + cw[1:2] * x + cw[2:3] * x_p1 + cw[3:4] * x_p2 + cb
        return carry

    lax.fori_loop(0, n_chunks, conv_body, 0)

    def affine(c0, d):
        rx = rx_s[pl.ds(c0, chunk), :]
        w = w_ref[0][:, d * 2 * RG_DH:(d + 1) * 2 * RG_DH]
        gates = jnp.dot(rx.astype(jnp.bfloat16), w, preferred_element_type=jnp.float32)
        gates = gates + gb_ref[0][:, d * 2 * RG_DH:(d + 1) * 2 * RG_DH]
        r = jax.nn.sigmoid(gates[:, :RG_DH])
        i = jax.nn.sigmoid(gates[:, RG_DH:])
        a = jnp.exp(ca_ref[d:d + 1, :] * r)
        b = jnp.sqrt(1.0 - a * a) * (i * rx)
        return a, b

    def scan_body(k, carry):
        h_f, h_b = carry
        cf = pl.multiple_of(k * chunk, chunk)
        cbk = pl.multiple_of((n_chunks - 1 - k) * chunk, chunk)
        a_f, b_f = affine(cf, 0)
        a_b, b_b = affine(cbk, 1)
        hh_f, h_f = _chunk_recurrence(a_f, b_f, h_f, reverse=False)
        hh_b, h_b = _chunk_recurrence(a_b, b_b, h_b, reverse=True)
        hf_s[pl.ds(cf, chunk), :] = hh_f
        hb_s[pl.ds(cbk, chunk), :] = hh_b
        return h_f, h_b

    h_f, h_b = lax.fori_loop(0, n_chunks, scan_body, (h0_ref[0][0:1, :], h0_ref[0][1:2, :]))
    hT_ref[0] = jnp.concatenate([h_f, h_b], axis=0)

    def out_body(k, carry):
        c0 = pl.multiple_of(k * chunk, chunk)
        y_ref[pl.ds(c0, chunk), :] = ((hf_s[pl.ds(c0, chunk), :] + hb_s[pl.ds(c0, chunk), :])
                                      * jax.nn.gelu(g_ref[pl.ds(c0, chunk), :]))
        return carry

    lax.fori_loop(0, n_chunks, out_body, 0)


RG_X_COL = 3 * HY_W
RG_GATE_COL = RG_X_COL + RG_W


def rglru(z2d, lp, h0, seq_len, chunk=128):
    M = z2d.shape[0]
    B = M // seq_len
    f32 = jnp.float32
    gw = lp['rg_gate_w']
    w = jnp.transpose(gw, (2, 3, 0, 1, 4)).reshape(RG_H, RG_DH, 4 * RG_DH).astype(jnp.bfloat16)
    gb = jnp.transpose(lp['rg_gate_b'].reshape(2, 2, RG_H, RG_DH), (2, 0, 1, 3)).reshape(RG_H, 1, 4 * RG_DH)
    ca = -RG_C * jax.nn.softplus(-lp['rg_lambda'].astype(f32))
    xcol = RG_X_COL // RG_DH
    gcol = RG_GATE_COL // RG_DH
    kern = functools.partial(_rglru_kernel, seq_len=seq_len, chunk=chunk)
    return pl.pallas_call(
        kern,
        grid=(B, RG_H),
        in_specs=[
            pl.BlockSpec((seq_len, RG_DH), lambda b, h: (b, xcol + h)),
            pl.BlockSpec((seq_len, RG_DH), lambda b, h: (b, gcol + h)),
            pl.BlockSpec((RG_CONV, RG_DH), lambda b, h: (0, h)),
            pl.BlockSpec((1, RG_DH), lambda b, h: (0, h)),
            pl.BlockSpec((1, RG_DH, 4 * RG_DH), lambda b, h: (h, 0, 0)),
            pl.BlockSpec((1, 1, 4 * RG_DH), lambda b, h: (h, 0, 0)),
            pl.BlockSpec((2, RG_DH), lambda b, h: (0, h)),
            pl.BlockSpec((1, 2, RG_DH), lambda b, h: (b, 0, h)),
        ],
        out_specs=[
            pl.BlockSpec((seq_len, RG_DH), lambda b, h: (b, h)),
            pl.BlockSpec((1, 2, RG_DH), lambda b, h: (b, 0, h)),
        ],
        out_shape=[jax.ShapeDtypeStruct((M, RG_W), f32), jax.ShapeDtypeStruct((B, 2, RG_W), f32)],
        scratch_shapes=[pltpu.VMEM((seq_len, RG_DH), f32), pltpu.VMEM((seq_len, RG_DH), f32),
                        pltpu.VMEM((seq_len, RG_DH), f32)],
        compiler_params=_compiler_params(("parallel", "parallel")),
        name="rglru",
    )(z2d, z2d, lp['rg_conv_w'], lp['rg_conv_b'].reshape(1, RG_W), w, gb, ca, h0.astype(f32))


HG_BLOCK = 256


def _chunk_cumsum(x, reverse):
    n = x.shape[0]
    r = lax.broadcasted_iota(jnp.int32, (n, 1), 0) % HG_CHUNK
    s = 1
    while s < HG_CHUNK:
        shift = n - s if reverse else s
        valid = (r < HG_CHUNK - s) if reverse else (r >= s)
        x = x + jnp.where(valid, pltpu.roll(x, shift, axis=0), 0.0)
        s *= 2
    return x


def _pair_levels(rows, reverse):
    r = lax.broadcasted_iota(jnp.int32, (rows, rows), 0)
    c = lax.broadcasted_iota(jnp.int32, (rows, rows), 1)
    level = jnp.zeros((rows, rows), jnp.int32)
    size = HG_CHUNK
    while size < rows:
        level = level + jnp.where((r // size) == (c // size), 0, 1)
        size *= 2
    return jnp.where((c < r) if reverse else (c > r), -1, level)


def _hgrn_block(q, v, graw, lb, s_t, level, reverse):
    bf16 = jnp.bfloat16
    rows = q.shape[0]
    nseg = rows // HG_CHUNK
    f = lb + (1.0 - lb) * jax.nn.sigmoid(graw)
    k = 1.0 - f
    bc = _chunk_cumsum(jnp.log(f), reverse)
    mid = HG_CHUNK // 2 - 1
    ref_row = HG_CHUNK - 1 - mid if reverse else mid
    tot_row = 0 if reverse else HG_CHUNK - 1
    nt = (((1,), (1,)), ((), ()))
    segs = [slice(j * HG_CHUNK, (j + 1) * HG_CHUNK) for j in range(nseg)]
    before = [None] * nseg
    acc = jnp.zeros_like(lb)
    for j in (range(nseg - 1, -1, -1) if reverse else range(nseg)):
        before[j] = acc
        acc = acc + bc[segs[j]][tot_row:tot_row + 1]
    total = acc
    full = lambda rows_: [jnp.broadcast_to(r_, (HG_CHUNK, r_.shape[1])) for r_ in rows_]
    g_blk = bc + jnp.concatenate(full(before), axis=0)
    ref = jnp.concatenate(full([bc[s][ref_row:ref_row + 1] for s in segs]), axis=0)
    att = lax.dot_general((q * jnp.exp(bc - ref)).astype(bf16), (k * jnp.exp(ref - bc)).astype(bf16), nt,
                          preferred_element_type=jnp.float32)
    att = jnp.where(level == 0, att, 0.0)
    lvl, group = 1, 2
    while group <= nseg:
        half = group // 2
        bound = [before[(j // group) * group + (half - 1 if reverse else half)] for j in range(nseg)]
        bound = jnp.concatenate(full(bound), axis=0)
        qf = (q * jnp.exp(jnp.minimum(g_blk - bound, 0.0))).astype(bf16)
        kf = (k * jnp.exp(jnp.minimum(bound - g_blk, 0.0))).astype(bf16)
        att = jnp.where(level == lvl, lax.dot_general(qf, kf, nt, preferred_element_type=jnp.float32), att)
        lvl, group = lvl + 1, group * 2
    v16 = v.astype(bf16)
    o = jnp.dot(att.astype(bf16), v16, preferred_element_type=jnp.float32)
    o = o + lax.dot_general((q * jnp.exp(g_blk)).astype(bf16), s_t.astype(bf16), nt,
                            preferred_element_type=jnp.float32)
    kv_t = lax.dot_general(v16, (k * jnp.exp(total - g_blk)).astype(bf16), (((0,), (0,)), ((), ())),
                           preferred_element_type=jnp.float32)
    return o, jnp.exp(total) * s_t + kv_t


def _hgrn_kernel(q_ref, v_ref, ff_ref, fb_ref, gate_ref, lb_ref, ng_ref, s0_ref, y_ref, sT_ref, of_s, ob_s,
                 *, seq_len, block):
    n_blocks = seq_len // block
    mask_f = _pair_levels(block, reverse=False)
    mask_b = _pair_levels(block, reverse=True)

    def scan_body(i, carry):
        s_f, s_b = carry
        cf = pl.multiple_of(i * block, block)
        cb = pl.multiple_of((n_blocks - 1 - i) * block, block)
        o_f, s_f = _hgrn_block(q_ref[pl.ds(cf, block), :], v_ref[pl.ds(cf, block), :], ff_ref[pl.ds(cf, block), :],
                               lb_ref[0:1, :], s_f, mask_f, reverse=False)
        o_b, s_b = _hgrn_block(q_ref[pl.ds(cb, block), :], v_ref[pl.ds(cb, block), :], fb_ref[pl.ds(cb, block), :],
                               lb_ref[1:2, :], s_b, mask_b, reverse=True)
        of_s[pl.ds(cf, block), :] = o_f
        ob_s[pl.ds(cb, block), :] = o_b
        return s_f, s_b

    s_f, s_b = lax.fori_loop(0, n_blocks, scan_body, (s0_ref[0, 0, 0].T, s0_ref[0, 1, 0].T))
    sT_ref[0, 0, 0] = s_f.T
    sT_ref[0, 1, 0] = s_b.T

    def out_body(i, carry):
        c0 = pl.multiple_of(i * block, block)
        o = of_s[pl.ds(c0, block), :] + ob_s[pl.ds(c0, block), :]
        inv = lax.rsqrt(jnp.mean(o * o, axis=-1, keepdims=True) + NORM_EPS)
        gate = gate_ref[pl.ds(c0, block), :]
        y_ref[pl.ds(c0, block), :] = o * inv * ng_ref[...] * (gate * jax.nn.sigmoid(gate))
        return carry

    lax.fori_loop(0, n_blocks, out_body, 0)


HG_Q_COL = RG_GATE_COL + RG_W


def hgrn2(z2d, lp, lb, s0, seq_len):
    M = z2d.shape[0]
    B = M // seq_len
    f32 = jnp.float32
    block = min(HG_BLOCK, seq_len)
    col = HG_Q_COL // HG_DK
    kern = functools.partial(_hgrn_kernel, seq_len=seq_len, block=block)

    def zspec(part):
        return pl.BlockSpec((seq_len, HG_DK), lambda b, h: (b, col + part * HG_H + h))

    return pl.pallas_call(
        kern,
        grid=(B, HG_H),
        in_specs=[
            zspec(0), zspec(1), zspec(2), zspec(3), zspec(4),
            pl.BlockSpec((2, HG_DK), lambda b, h: (0, h)),
            pl.BlockSpec((1, HG_DV), lambda b, h: (0, h)),
            pl.BlockSpec((1, 2, 1, HG_DK, HG_DV), lambda b, h: (b, 0, h, 0, 0)),
        ],
        out_specs=[
            pl.BlockSpec((seq_len, HG_DV), lambda b, h: (b, h)),
            pl.BlockSpec((1, 2, 1, HG_DK, HG_DV), lambda b, h: (b, 0, h, 0, 0)),
        ],
        out_shape=[jax.ShapeDtypeStruct((M, HG_W), f32), jax.ShapeDtypeStruct((B, 2, HG_H, HG_DK, HG_DV), f32)],
        scratch_shapes=[pltpu.VMEM((seq_len, HG_DV), f32), pltpu.VMEM((seq_len, HG_DV), f32)],
        compiler_params=_compiler_params(("parallel", "parallel")),
        name="hgrn2",
    )(z2d, z2d, z2d, z2d, z2d, lb.astype(f32), lp['hg_norm_g'].reshape(1, HG_W), s0.astype(f32))


ATT_Q_COL = HG_Q_COL + 5 * HG_W
ATT_KV_COL = ATT_Q_COL + ATT_W
LANES = 128
ATT_TQ = 1024


def _head_mean_square(x):
    ss = x * x
    hi = ss.astype(jnp.bfloat16)
    lo = (ss - hi.astype(jnp.float32)).astype(jnp.bfloat16)
    r = lax.broadcasted_iota(jnp.int32, (LANES, LANES), 0) // HEAD_DIM
    c = lax.broadcasted_iota(jnp.int32, (LANES, LANES), 1) // HEAD_DIM
    ones = jnp.where(r == c, 1.0, 0.0).astype(jnp.bfloat16)
    tot = jnp.dot(hi, ones, preferred_element_type=jnp.float32) + jnp.dot(lo, ones, preferred_element_type=jnp.float32)
    return tot * (1.0 / HEAD_DIM)


def _qk_norm_rope(x, gain, cos, sin):
    y = x * lax.rsqrt(_head_mean_square(x) + NORM_EPS) * gain
    if cos is None:
        return y
    lane = lax.broadcasted_iota(jnp.int32, (1, LANES), 1) % (HEAD_DIM // 2)
    quarter = HEAD_DIM // 4
    partner = jnp.where(lane < quarter, pltpu.roll(y, LANES - quarter, axis=1), pltpu.roll(y, quarter, axis=1))
    return y * cos + partner * sin


def _both_halves(x, g):
    lane = lax.broadcasted_iota(jnp.int32, (1, LANES), 1)
    keep = (lane < HEAD_DIM) if g == 0 else (lane >= HEAD_DIM)
    return jnp.where(keep, x, pltpu.roll(x, HEAD_DIM, axis=1))


def _attn_kernel(*refs, latent, tq, seq_len):
    bf16 = jnp.bfloat16
    if latent:
        (q_ref, kvm_ref, kvp_ref, kvn_ref, cosm_ref, sinm_ref, cosp_ref, sinp_ref, cosn_ref, sinn_ref,
         ck_ref, cv_ref, qg_ref, kg_ref, sink_ref, y_ref, q_s, k2_s, v2_s, ck2_s, cv2_s) = refs
    else:
        (q_ref, kvm_ref, qg_ref, kg_ref, sink_ref, y_ref, ka_ref, q_s, k2_s, v2_s) = refs
    i = pl.program_id(1)
    blk = ATT_BLOCK

    if latent:
        pieces = [(kvp_ref, cosp_ref, sinp_ref, 0, blk), (kvm_ref, cosm_ref, sinm_ref, blk, tq),
                  (kvn_ref, cosn_ref, sinn_ref, blk + tq, blk)]
    else:
        pieces = [(kvm_ref, None, None, 0, tq)]
    for kv_ref, cos_ref, sin_ref, r0, n in pieces:
        kv = kv_ref[...]
        kn = _qk_norm_rope(kv[:, :LANES], kg_ref[...], None if cos_ref is None else cos_ref[...],
                           None if sin_ref is None else sin_ref[...])
        if not latent:
            ka_ref[...] = kn
        for g in range(ATT_KV):
            k2_s[g, r0:r0 + n, :] = _both_halves(kn, g).astype(bf16)
            v2_s[g, r0:r0 + n, :] = _both_halves(kv[:, LANES:], g).astype(bf16)
    if latent:
        for g in range(ATT_KV):
            ck2_s[g] = _both_halves(ck_ref[0], g).astype(bf16)
            cv2_s[g] = _both_halves(cv_ref[0], g).astype(bf16)

    for p in range(ATT_W // LANES):
        qn = _qk_norm_rope(q_ref[:, p * LANES:(p + 1) * LANES], qg_ref[...],
                           cosm_ref[...] if latent else None, sinm_ref[...] if latent else None)
        q_s[:, p * LANES:(p + 1) * LANES] = (qn * ATT_SCALE).astype(bf16)

    rows = ATT_G * blk
    lane = lax.broadcasted_iota(jnp.int32, (1, LANES), 1)
    low = lane < HEAD_DIM
    head_of_row = lax.broadcasted_iota(jnp.int32, (rows, 1), 0) // blk
    nt = (((1,), (1,)), ((), ()))
    for j in range(tq // blk):
        for g in range(ATT_KV):
            stack = []
            for p in (2 * g, 2 * g + 1):
                qb = q_s[j * blk:(j + 1) * blk, p * LANES:(p + 1) * LANES]
                stack.append(jnp.where(low, qb, jnp.zeros_like(qb)))
                stack.append(jnp.where(low, jnp.zeros_like(qb), qb))
            qg = jnp.concatenate(stack, axis=0)
            sink = jnp.zeros((rows, 1), jnp.float32)
            for hh in range(ATT_G):
                sink = jnp.where(head_of_row == hh, sink_ref[ATT_G * g + hh:ATT_G * g + hh + 1, 0:1], sink)
            if latent:
                nk = 3 * blk
                s_w = lax.dot_general(qg, k2_s[g, j * blk:j * blk + nk, :], nt, preferred_element_type=jnp.float32)
                a = lax.broadcasted_iota(jnp.int32, (rows, nk), 0) % blk
                c = lax.broadcasted_iota(jnp.int32, (rows, nk), 1)
                kpos = c + (i * tq + j * blk - blk)
                ok = (jnp.abs(c - blk - a) <= WINDOW) & (kpos >= 0) & (kpos < seq_len)
                s_w = jnp.where(ok, s_w, NEG_INF)
                s_c = lax.dot_general(qg, ck2_s[g], nt, preferred_element_type=jnp.float32)
                m = jnp.maximum(jnp.maximum(jnp.max(s_w, axis=-1, keepdims=True),
                                            jnp.max(s_c, axis=-1, keepdims=True)), sink)
                e_w = jnp.exp(s_w - m)
                e_c = jnp.exp(s_c - m)
                den = jnp.sum(e_w, axis=-1, keepdims=True) + jnp.sum(e_c, axis=-1, keepdims=True) + jnp.exp(sink - m)
                o = (jnp.dot(e_w.astype(bf16), v2_s[g, j * blk:j * blk + nk, :], preferred_element_type=jnp.float32)
                     + jnp.dot(e_c.astype(bf16), cv2_s[g], preferred_element_type=jnp.float32))
            else:
                s = lax.dot_general(qg, k2_s[g], nt, preferred_element_type=jnp.float32)
                m = jnp.maximum(jnp.max(s, axis=-1, keepdims=True), sink)
                e = jnp.exp(s - m)
                den = jnp.sum(e, axis=-1, keepdims=True) + jnp.exp(sink - m)
                o = jnp.dot(e.astype(bf16), v2_s[g], preferred_element_type=jnp.float32)
            o = o / den
            for t, p in enumerate((2 * g, 2 * g + 1)):
                y_ref[j * blk:(j + 1) * blk, p * LANES:(p + 1) * LANES] = jnp.where(
                    low, o[2 * t * blk:(2 * t + 1) * blk], o[(2 * t + 1) * blk:(2 * t + 2) * blk])


def _rope_tables(seq_len):
    f32 = jnp.float32
    half = HEAD_DIM // 2
    t = jnp.arange(seq_len)
    row = (t // GRID_W).astype(f32)
    col = (t % GRID_W).astype(f32)
    inv = ROPE_BASE ** (-jnp.arange(0, half, 2, dtype=f32) / half)

    def tab(pos):
        ang = pos[:, None] * inv[None, :]
        return (jnp.concatenate([jnp.cos(ang), jnp.cos(ang)], axis=-1),
                jnp.concatenate([-jnp.sin(ang), jnp.sin(ang)], axis=-1))

    cr, sr = tab(row)
    cc, sc = tab(col)
    cos = jnp.concatenate([cr, cc], axis=-1)
    sin = jnp.concatenate([sr, sc], axis=-1)
    return jnp.tile(cos, (1, 2)), jnp.tile(sin, (1, 2))


def attention(z2d, lp, seq_len, ctx_k=None, ctx_v=None):
    M = z2d.shape[0]
    B = M // seq_len
    f32 = jnp.float32
    bf16 = jnp.bfloat16
    latent = ctx_k is not None
    tq = min(ATT_TQ, seq_len)
    nq = seq_len // tq
    qg = jnp.tile(lp['att_qn_g'].astype(f32), 2).reshape(1, LANES)
    kg = jnp.tile(lp['att_kn_g'].astype(f32), 2).reshape(1, LANES)
    sink = jnp.broadcast_to(lp['att_sink'].astype(f32)[:, None], (ATT_H, LANES))
    qcol = ATT_Q_COL // ATT_W
    kvcol = ATT_KV_COL // (2 * LANES)
    kern = functools.partial(_attn_kernel, latent=latent, tq=tq, seq_len=seq_len)
    q_spec = pl.BlockSpec((tq, ATT_W), lambda b, i: (b * nq + i, qcol))
    kv_spec = pl.BlockSpec((tq, 2 * LANES), lambda b, i: (b * nq + i, kvcol))
    small = lambda shape: pl.BlockSpec(shape, lambda b, i: (0, 0))
    y_spec = pl.BlockSpec((tq, ATT_W), lambda b, i: (b * nq + i, 0))
    params = _compiler_params(("parallel", "parallel"))
    if not latent:
        return pl.pallas_call(
            kern, grid=(B, nq),
            in_specs=[q_spec, kv_spec, small((1, LANES)), small((1, LANES)), small((ATT_H, LANES))],
            out_specs=[y_spec, pl.BlockSpec((tq, LANES), lambda b, i: (b * nq + i, 0))],
            out_shape=[jax.ShapeDtypeStruct((M, ATT_W), f32), jax.ShapeDtypeStruct((M, LANES), f32)],
            scratch_shapes=[pltpu.VMEM((tq, ATT_W), bf16), pltpu.VMEM((ATT_KV, tq, LANES), bf16),
                            pltpu.VMEM((ATT_KV, tq, LANES), bf16)],
            compiler_params=params, name="attn_context",
        )(z2d, z2d, qg, kg, sink)
    P = ctx_k.shape[1]
    hb = tq // ATT_BLOCK
    nblk = seq_len // ATT_BLOCK
    cos, sin = _rope_tables(seq_len)
    prev_idx = lambda b, i: (b * nblk + jnp.maximum(i * hb - 1, 0), kvcol)
    next_idx = lambda b, i: (b * nblk + jnp.minimum((i + 1) * hb, nblk - 1), kvcol)
    tab_m = pl.BlockSpec((tq, LANES), lambda b, i: (i, 0))
    tab_p = pl.BlockSpec((ATT_BLOCK, LANES), lambda b, i: (jnp.maximum(i * hb - 1, 0), 0))
    tab_n = pl.BlockSpec((ATT_BLOCK, LANES), lambda b, i: (jnp.minimum((i + 1) * hb, nblk - 1), 0))
    ctx_spec = pl.BlockSpec((1, P, LANES), lambda b, i: (b, 0, 0))
    nk_all = tq + 2 * ATT_BLOCK
    return pl.pallas_call(
        kern, grid=(B, nq),
        in_specs=[q_spec, kv_spec, pl.BlockSpec((ATT_BLOCK, 2 * LANES), prev_idx),
                  pl.BlockSpec((ATT_BLOCK, 2 * LANES), next_idx),
                  tab_m, tab_m, tab_p, tab_p, tab_n, tab_n, ctx_spec, ctx_spec,
                  small((1, LANES)), small((1, LANES)), small((ATT_H, LANES))],
        out_specs=y_spec,
        out_shape=jax.ShapeDtypeStruct((M, ATT_W), f32),
        scratch_shapes=[pltpu.VMEM((tq, ATT_W), bf16), pltpu.VMEM((ATT_KV, nk_all, LANES), bf16),
                        pltpu.VMEM((ATT_KV, nk_all, LANES), bf16), pltpu.VMEM((ATT_KV, P, LANES), bf16),
                        pltpu.VMEM((ATT_KV, P, LANES), bf16)],
        compiler_params=params, name="attn_latent",
    )(z2d, z2d, z2d, z2d, cos, sin, cos, sin, cos, sin, ctx_k.astype(f32), ctx_v.astype(f32), qg, kg, sink)


HY_CHUNK = 256
HY_TILE_F = 256
DFT_SPLIT = 64


def _hy_pre_kernel(hv_ref, h1_ref, h2_ref, cw_ref, cb_ref, skip_ref, w_ref, x1_ref, base_ref, *, seq_len, chunk):
    n_chunks = seq_len // chunk
    row = lax.broadcasted_iota(jnp.int32, (chunk, 1), 0)

    def conv(ref, part, c0):
        x = ref[pl.ds(c0, chunk), :]
        before = ref[pl.ds(pl.multiple_of(jnp.maximum(c0 - 8, 0), 8), 8), :]
        after = ref[pl.ds(pl.multiple_of(jnp.minimum(c0 + chunk, seq_len - 8), 8), 8), :]
        pos = row + c0
        x_m1 = jnp.where(row == 0, before[7:8], pltpu.roll(x, 1, axis=0))
        x_p1 = jnp.where(row == chunk - 1, after[0:1], pltpu.roll(x, chunk - 1, axis=0))
        x_m1 = jnp.where(pos >= 1, x_m1, 0.0)
        x_p1 = jnp.where(pos < seq_len - 1, x_p1, 0.0)
        cw = cw_ref[part]
        return cw[0:1] * x_m1 + cw[1:2] * x + cw[2:3] * x_p1 + cb_ref[part]

    def body(k, carry):
        c0 = pl.multiple_of(k * chunk, chunk)
        hv = conv(hv_ref, 0, c0)
        hx1 = conv(h1_ref, 1, c0)
        hx2 = conv(h2_ref, 2, c0)
        w = hv * hx2
        w_ref[pl.ds(c0, chunk), :] = w.astype(jnp.bfloat16)
        x1_ref[pl.ds(c0, chunk), :] = hx1
        base_ref[pl.ds(c0, chunk), :] = hx1 * (w * skip_ref[...])
        return carry

    lax.fori_loop(0, n_chunks, body, 0)


def hyena_pre(z2d, lp, seq_len):
    M = z2d.shape[0]
    B = M // seq_len
    f32 = jnp.float32
    nblk = HY_W // LANES
    chunk = min(HY_CHUNK, seq_len)
    cw = lp['hy_conv_w'].reshape(HY_SHORT, 3, HY_W).transpose(1, 0, 2)
    cb = lp['hy_conv_b'].reshape(3, 1, HY_W)
    kern = functools.partial(_hy_pre_kernel, seq_len=seq_len, chunk=chunk)
    zspec = lambda part: pl.BlockSpec((seq_len, LANES), lambda b, c: (b, part * nblk + c))
    ospec = pl.BlockSpec((seq_len, LANES), lambda b, c: (b, c))
    return pl.pallas_call(
        kern, grid=(B, nblk),
        in_specs=[zspec(0), zspec(1), zspec(2),
                  pl.BlockSpec((3, HY_SHORT, LANES), lambda b, c: (0, 0, c)),
                  pl.BlockSpec((3, 1, LANES), lambda b, c: (0, 0, c)),
                  pl.BlockSpec((1, LANES), lambda b, c: (0, c))],
        out_specs=[ospec, ospec, ospec],
        out_shape=[jax.ShapeDtypeStruct((M, HY_W), jnp.bfloat16), jax.ShapeDtypeStruct((M, HY_W), f32),
                   jax.ShapeDtypeStruct((M, HY_W), f32)],
        compiler_params=_compiler_params(("parallel", "parallel")),
        name="hyena_pre",
    )(z2d, z2d, z2d, cw, cb, lp['hy_skip'].reshape(1, HY_W).astype(f32))


def _hy_conv_kernel(g_ref, gi_ref, w_ref, kra_ref, krb_ref, ki_ref, o_ref, *, tf, seq_len):
    j = pl.program_id(1)
    x = jnp.dot(g_ref[0], w_ref[...], preferred_element_type=jnp.float32)
    xr, xs = x[:tf], x[tf:]
    ki = ki_ref[...]
    a = xr * kra_ref[...] + xs * ki
    b = xs * krb_ref[...] - xr * ki
    ab = (jnp.concatenate([a, b], axis=0) * (1.0 / seq_len)).astype(jnp.bfloat16)
    part = jnp.dot(gi_ref[0], ab, preferred_element_type=jnp.float32)

    @pl.when(j == 0)
    def _():
        o_ref[...] = part

    @pl.when(j > 0)
    def _():
        o_ref[...] += part


def _dft_tables(seq_len, tf):
    n = 2 * seq_len
    nf = seq_len // tf
    f = jnp.arange(seq_len, dtype=jnp.int32)

    def small(freqs):
        ang = ((freqs[:, None] * f[None, :]) % n).astype(jnp.float32) * (2.0 * math.pi / n)
        return jnp.cos(ang), jnp.sin(ang)

    c1, s1 = small(DFT_SPLIT * jnp.arange(seq_len // DFT_SPLIT, dtype=jnp.int32))
    c0, s0 = small(jnp.arange(DFT_SPLIT, dtype=jnp.int32))
    c = (c1[:, None, :] * c0[None, :, :] - s1[:, None, :] * s0[None, :, :]).reshape(seq_len, seq_len)
    s = (s1[:, None, :] * c0[None, :, :] + c1[:, None, :] * s0[None, :, :]).reshape(seq_len, seq_len)
    alt = (1 - 2 * (f % 2)).astype(jnp.float32)
    s = jnp.where(f[:, None] == 0, alt[None, :], s)
    fwd = jnp.concatenate([c.reshape(nf, tf, seq_len), s.reshape(nf, tf, seq_len)], axis=1)
    inv = jnp.concatenate([c.reshape(seq_len, nf, tf), s.T.reshape(seq_len, nf, tf)], axis=2).transpose(1, 0, 2)
    return fwd.astype(jnp.bfloat16), inv.astype(jnp.bfloat16)


def hyena_long_conv(w_bf16, k_long, seq_len):
    M, W = w_bf16.shape
    B = M // seq_len
    f32 = jnp.float32
    tf = min(HY_TILE_F, seq_len)
    nf = seq_len // tf
    fwd, inv = _dft_tables(seq_len, tf)
    kf = jnp.fft.rfft(k_long, axis=0)
    kr = jnp.real(kf[:seq_len]).astype(f32)
    ki = jnp.imag(kf[:seq_len]).astype(f32).at[0].set(0.0)
    kra = kr.at[0].multiply(0.5)
    krb = kr.at[0].set(0.5 * jnp.real(kf[seq_len]).astype(f32))
    kern = functools.partial(_hy_conv_kernel, tf=tf, seq_len=seq_len)
    kspec = pl.BlockSpec((tf, W), lambda b, j: (j, 0))
    return pl.pallas_call(
        kern, grid=(B, nf),
        in_specs=[pl.BlockSpec((1, 2 * tf, seq_len), lambda b, j: (j, 0, 0)),
                  pl.BlockSpec((1, seq_len, 2 * tf), lambda b, j: (j, 0, 0)),
                  pl.BlockSpec((seq_len, W), lambda b, j: (b, 0)),
                  kspec, kspec, kspec],
        out_specs=pl.BlockSpec((seq_len, W), lambda b, j: (b, 0)),
        out_shape=jax.ShapeDtypeStruct((M, W), f32),
        compiler_params=_compiler_params(("parallel", "arbitrary")),
        name="hyena_long_conv",
    )(fwd, inv, w_bf16, kra, krb, ki)


def hyena_kernel(L, w1, b1, freq, w2, b2, w3, b3, decay):
    f32 = jnp.float32
    t_idx = jnp.arange(L, dtype=f32)
    t = t_idx / (L - 1)
    ang = (2.0 * math.pi * t_idx / L)[:, None] * jnp.linspace(1e-4, HY_BANDS - 1, HY_BANDS, dtype=f32)[None, :]
    z = jnp.concatenate([t[:, None], jnp.cos(ang), -jnp.sin(ang)], axis=-1)
    h = jnp.sin(freq[0].astype(f32) * (z @ w1.astype(f32) + b1.astype(f32)))
    h = jnp.sin(freq[1].astype(f32) * (h @ w2.astype(f32) + b2.astype(f32)))
    h = (h @ w3.astype(f32) + b3.astype(f32)) * jnp.exp(-t[:, None] * jnp.abs(decay.astype(f32)))
    h_fwd, h_bwd = h[:, :HY_W], h[:, HY_W:]
    k = jnp.concatenate([h_fwd, jnp.zeros((1, HY_W), f32), h_bwd[:0:-1]], axis=0)
    return k / jnp.sum(jnp.abs(k), axis=0, keepdims=True)


def token_mixers(z, lp, lb, ctx):
    f32 = jnp.float32
    B, T, _ = z.shape
    z2d = z.reshape(B * T, -1)

    k_long = hyena_kernel(T, lp['hy_f_w1'], lp['hy_f_b1'], lp['hy_f_freq'], lp['hy_f_w2'], lp['hy_f_b2'],
                          lp['hy_f_w3'], lp['hy_f_b3'], lp['hy_decay'])
    hy_w, hy_x1, hy_base = hyena_pre(z2d, lp, T)
    hy_conv = hyena_long_conv(hy_w, k_long, T)

    h0 = jnp.zeros((B, 2, RG_W), f32) if ctx is None else ctx[2]
    y_rg, hT_rg = rglru(z2d, lp, h0, T)

    S0 = jnp.zeros((B, 2, HG_H, HG_DK, HG_DV), f32) if ctx is None else ctx[3]
    y_hg, S_hg = hgrn2(z2d, lp, lb, S0, T)

    if ctx is None:
        y_att, ka = attention(z2d, lp, T)
        ka = ka.reshape(B, T, ATT_KV, HEAD_DIM)
        va = z[..., ATT_KV_COL + LANES:ATT_KV_COL + 2 * LANES].reshape(B, T, ATT_KV, HEAD_DIM)
        new_ctx = (ka, va, hT_rg, S_hg)
    else:
        P = ctx[0].shape[1]
        y_att = attention(z2d, lp, T, ctx[0].reshape(B, P, ATT_KV * HEAD_DIM), ctx[1].reshape(B, P, ATT_KV * HEAD_DIM))
        new_ctx = None
    return (hy_conv, hy_x1, hy_base, y_rg, y_hg, y_att), new_ctx


def trunk_layer(x, mod, lp, lb, ctx):
    B, T, D = x.shape
    sh1, sc1, g1, sh2, sc2, g2 = [m[:, None, :] for m in jnp.split(mod, 6, axis=-1)]
    gs1 = lp['norm1_g'][None, None, :] * (1.0 + sc1)
    gs2 = lp['norm2_g'][None, None, :] * (1.0 + sc2)
    x2d = x.reshape(B * T, D)
    z = in_proj(x2d, gs1, sh1, lp['w_in_bf16'], T)
    mix_parts, new_ctx = token_mixers(z.reshape(B, T, IN_COLS_PAD), lp, lb, ctx)
    x1, u2 = out_proj(mix_parts, lp['w_out_bf16'], x2d, g1, gs2, sh2, T)
    x2 = conv_ffn(u2, lp['ffn_up_bf16'], lp['ffn_conv_w'], lp['ffn_conv_b'], lp['ffn_down_bf16'], x1, g2, T)
    return x2.reshape(B, T, D), new_ctx


def kernel(x_prompt, x_sample, cache_k, cache_v, state_rglru, state_hgrn, c, c_ctx, norm1_g, norm2_g, ada_w, ada_b, w_in, w_out, hy_conv_w, hy_conv_b, hy_f_w1, hy_f_b1, hy_f_freq, hy_f_w2, hy_f_b2, hy_f_w3, hy_f_b3, hy_decay, hy_skip, rg_conv_w, rg_conv_b, rg_gate_w, rg_gate_b, rg_lambda, hg_lb, hg_norm_g, att_qn_g, att_kn_g, att_sink, ffn_up, ffn_conv_w, ffn_conv_b, ffn_down):
    bf16 = jnp.bfloat16
    stacked = dict(norm1_g=norm1_g, norm2_g=norm2_g, ada_w=ada_w, ada_b=ada_b,
                   hy_conv_w=hy_conv_w, hy_conv_b=hy_conv_b, hy_f_w1=hy_f_w1, hy_f_b1=hy_f_b1, hy_f_freq=hy_f_freq,
                   hy_f_w2=hy_f_w2, hy_f_b2=hy_f_b2, hy_f_w3=hy_f_w3, hy_f_b3=hy_f_b3, hy_decay=hy_decay, hy_skip=hy_skip,
                   rg_conv_w=rg_conv_w, rg_conv_b=rg_conv_b, rg_gate_w=rg_gate_w, rg_gate_b=rg_gate_b, rg_lambda=rg_lambda,
                   hg_norm_g=hg_norm_g, att_qn_g=att_qn_g, att_kn_g=att_kn_g, att_sink=att_sink,
                   ffn_conv_w=ffn_conv_w, ffn_conv_b=ffn_conv_b,
                   w_in_bf16=jnp.pad(w_in.astype(bf16), ((0, 0), (0, 0), (0, IN_COLS_PAD - IN_COLS))),
                   w_out_bf16=w_out.astype(bf16), ffn_up_bf16=ffn_up.astype(bf16), ffn_down_bf16=ffn_down.astype(bf16))
    lbs = jnp.cumsum(jax.nn.softmax(hg_lb.astype(jnp.float32), axis=0), axis=0)
    lbs = lbs - lbs[0:1]
    y_prompt, y_sample = x_prompt, x_sample
    ks, vs, rgs, hgs = [], [], [], []
    for l in range(DEPTH):
        lp = {name: arr[l] for name, arr in stacked.items()}
        mod_p = (jax.nn.silu(c_ctx) @ lp['ada_w'] + lp['ada_b'])[None, :]
        y_prompt, ctx_new = trunk_layer(y_prompt, mod_p, lp, lbs[l], None)
        ks.append(ctx_new[0])
        vs.append(ctx_new[1])
        rgs.append(ctx_new[2])
        hgs.append(ctx_new[3])
        mod_s = jax.nn.silu(c) @ lp['ada_w'] + lp['ada_b']
        cached = (cache_k[:, l], cache_v[:, l], state_rglru[:, l], state_hgrn[:, l])
        y_sample, _ = trunk_layer(y_sample, mod_s, lp, lbs[l], cached)
    return (y_prompt, y_sample, jnp.stack(ks, axis=1), jnp.stack(vs, axis=1),
            jnp.stack(rgs, axis=1), jnp.stack(hgs, axis=1))
```
